```python
import jax, jax.numpy as jnp
from jax import lax
import numpy as np

D_MODEL = 4096
BATCH = 8
SEQ = 2048
DEPTH = 2
DEC_BATCH = 4
DEC_SEQ = 2048
PAST_LEN = 128

ATT_HEAD_DIM = 128
ATT_HEADS = 8
ATT_GROUPS = ((128, 1), (512, 4), (2048, 16))
N_ATT_GROUPS = len(ATT_GROUPS)
ATT_GROUP_WIDTH = ATT_HEADS * ATT_HEAD_DIM
ROPE_THETA = 10000.0
RET_HEADS = 8
RET_QK_DIM = 128
RET_V_DIM = 256
RET_CHUNK = 128
RET_QK_WIDTH = RET_HEADS * RET_QK_DIM
RET_V_WIDTH = RET_HEADS * RET_V_DIM
RET_THETA = 10000.0
ATT_COLS = 3 * N_ATT_GROUPS * ATT_GROUP_WIDTH
RET_COLS = 2 * RET_QK_WIDTH + 2 * RET_V_WIDTH
IN_COLS = ATT_COLS + RET_COLS
PEER_HEADS = 8
PEER_KEY_DIM = 128
N_KEYS = 128
N_EXPERTS = N_KEYS * N_KEYS
PEER_TOPK = 16
EPS = 1e-6
NEG_INF = -1e30

kernel_name = "hybrid_dilated_retention_peer_encoder"


def rmsnorm(x, g):
    xf = x.astype(jnp.float32)
    y = xf * lax.rsqrt(jnp.mean(xf * xf, axis=-1, keepdims=True) + EPS)
    return (y * g.astype(jnp.float32)).astype(x.dtype)


def rotary(x):
    s, dh = x.shape[1], x.shape[-1]
    inv = ROPE_THETA ** (-jnp.arange(0, dh, 2, dtype=jnp.float32) / dh)
    ang = jnp.arange(s, dtype=jnp.float32)[:, None] * inv[None, :]
    cos = jnp.cos(ang)[None, :, None, :]
    sin = jnp.sin(ang)[None, :, None, :]
    xf = x.astype(jnp.float32)
    x1, x2 = xf[..., : dh // 2], xf[..., dh // 2:]
    return jnp.concatenate([x1 * cos - x2 * sin, x1 * sin + x2 * cos], axis=-1).astype(x.dtype)


def retnet_rotation(x):
    s, dh = x.shape[1], x.shape[-1]
    inv = 1.0 / (RET_THETA ** jnp.linspace(0.0, 1.0, dh // 2, dtype=jnp.float32))
    ang = jnp.arange(s, dtype=jnp.float32)[:, None] * inv[None, :]
    cos = jnp.cos(ang)[None, :, None, :]
    sin = jnp.sin(ang)[None, :, None, :]
    xf = x.astype(jnp.float32).reshape(x.shape[:-1] + (dh // 2, 2))
    xe, xo = xf[..., 0], xf[..., 1]
    return jnp.stack([xe * cos - xo * sin, xe * sin + xo * cos], axis=-1).reshape(x.shape)


def banded_attention(q, k, v, w):
    n, l, h, dh = q.shape
    nb = -(-l // w)
    lp = nb * w
    qb = jnp.pad(q, ((0, 0), (0, lp - l), (0, 0), (0, 0))).reshape(n, nb, w, h, dh)
    pad_kv = ((0, 0), (w, lp - l + w), (0, 0), (0, 0))
    kp = jnp.pad(k, pad_kv)
    vp = jnp.pad(v, pad_kv)
    idx = jnp.arange(nb)[:, None] * w + jnp.arange(3 * w)[None, :]
    kb = kp[:, idx]
    vb = vp[:, idx]
    s = jnp.einsum('nbqhd,nbkhd->nbhqk', qb, kb).astype(jnp.float32) * (dh ** -0.5)
    qpos = jnp.arange(nb)[:, None] * w + jnp.arange(w)[None, :]
    kpos = idx - w
    rel = kpos[:, None, :] - qpos[:, :, None]
    valid = (jnp.abs(rel) <= w) & (kpos[:, None, :] >= 0) & (kpos[:, None, :] < l)
    s = jnp.where(valid[None, :, None], s, NEG_INF)
    lse = jax.nn.logsumexp(s, axis=-1)
    p = jnp.exp(s - lse[..., None])
    o = jnp.einsum('nbhqk,nbkhd->nbqhd', p.astype(v.dtype), vb).reshape(n, lp, h, dh)[:, :l]
    lse = jnp.moveaxis(lse, 2, 3).reshape(n, lp, h)[:, :l]
    return o, lse


def dilated_group(q, k, v, window, dilation):
    b, s, h, dh = q.shape
    l = s // dilation

    def to_sub(t):
        return t.reshape(b, l, dilation, h, dh).transpose(0, 2, 1, 3, 4).reshape(b * dilation, l, h, dh)

    o, lse = banded_attention(to_sub(q), to_sub(k), to_sub(v), window // (2 * dilation))
    o = o.reshape(b, dilation, l, h, dh).transpose(0, 2, 1, 3, 4).reshape(b, s, h, dh)
    lse = lse.reshape(b, dilation, l, h).transpose(0, 2, 1, 3).reshape(b, s, h)
    return o, lse


def dilated_attention_mixer(qa, ka, va):
    b, s = qa.shape[0], qa.shape[1]
    outs, lses = [], []
    for gi, (window, dilation) in enumerate(ATT_GROUPS):
        o, lse = dilated_group(rotary(qa[:, :, gi]), rotary(ka[:, :, gi]), va[:, :, gi], window, dilation)
        outs.append(o)
        lses.append(lse)
    wts = jax.nn.softmax(jnp.stack(lses), axis=0)
    o = jnp.sum(wts[..., None] * jnp.stack(outs).astype(jnp.float32), axis=0)
    return o.astype(qa.dtype).reshape(b, s, ATT_GROUP_WIDTH)


def retention_direction(q, k, v, log_gamma, strict):
    b, s, h, dk = q.shape
    dv = v.shape[-1]
    c = RET_CHUNK
    nc = s // c

    def chunks(t):
        return t.reshape(b, nc, c, h, t.shape[-1]).transpose(1, 0, 3, 2, 4)

    qc, kc, vc = chunks(q), chunks(k), chunks(v)
    n = jnp.arange(c, dtype=jnp.float32)
    rel = n[:, None] - n[None, :]
    mask = (rel > 0) if strict else (rel >= 0)
    decay_mat = jnp.where(mask[None], jnp.exp(jnp.maximum(rel, 0.0)[None] * log_gamma[:, None, None]), 0.0)
    intra = jnp.einsum('xbhnd,xbhmd->xbhnm', qc, kc) * decay_mat
    intra = jnp.einsum('xbhnm,xbhme->xbhne', intra, vc)
    query_decay = jnp.exp((n[None, :] + 1.0) * log_gamma[:, None])
    key_decay = jnp.exp((c - 1.0 - n)[None, :] * log_gamma[:, None])
    chunk_decay = jnp.exp(c * log_gamma)

    def step(state, inp):
        qi, ki, vi = inp
        cross = jnp.einsum('bhnd,bhde->bhne', qi, state) * query_decay[None, :, :, None]
        state = state * chunk_decay[None, :, None, None] + jnp.einsum(
            'bhmd,bhme->bhde', ki * key_decay[None, :, :, None], vi)
        return state, cross

    _, cross = lax.scan(step, jnp.zeros((b, h, dk, dv), jnp.float32), (qc, kc, vc))
    out = intra + cross
    return out.transpose(1, 0, 3, 2, 4).reshape(b, s, h, dv)


def retention_mixer(qr, kr, vr, gr, decay_fwd, decay_bwd):
    b, s = qr.shape[0], qr.shape[1]
    q = retnet_rotation(qr)
    k = retnet_rotation(kr) * (RET_QK_DIM ** -0.5)
    v = vr.astype(jnp.float32)
    lg_f = jax.nn.log_sigmoid(decay_fwd.astype(jnp.float32))
    lg_b = jax.nn.log_sigmoid(decay_bwd.astype(jnp.float32))
    fwd = retention_direction(q, k, v, lg_f, False)
    bwd = jnp.flip(retention_direction(jnp.flip(q, 1), jnp.flip(k, 1), jnp.flip(v, 1), lg_b, True), 1)
    y = fwd + bwd
    y = y * lax.rsqrt(jnp.mean(y * y, axis=-1, keepdims=True) + EPS)
    y = y.reshape(b, s, RET_V_WIDTH)
    return (jax.nn.silu(gr.astype(jnp.float32)) * y).astype(qr.dtype)


def peer_ffn(x, w_q, sub_keys, u, v):
    b, s, d = x.shape
    t = b * s
    xf = x.reshape(t, d)
    q = (xf @ w_q).reshape(t, PEER_HEADS, 2, PEER_KEY_DIM)
    scores = jnp.einsum('thpd,hpnd->thpn', q, sub_keys).astype(jnp.float32)
    top_s, top_i = lax.top_k(scores, PEER_TOPK)
    cand_s = (top_s[:, :, 0, :, None] + top_s[:, :, 1, None, :]).reshape(t, PEER_HEADS, PEER_TOPK * PEER_TOPK)
    cand_i = (top_i[:, :, 0, :, None] * N_KEYS + top_i[:, :, 1, None, :]).reshape(t, PEER_HEADS, PEER_TOPK * PEER_TOPK)
    sel_s, sel_pos = lax.top_k(cand_s, PEER_TOPK)
    expert = jnp.take_along_axis(cand_i, sel_pos, axis=-1).reshape(t, PEER_HEADS * PEER_TOPK)
    gate = jax.nn.softmax(sel_s, axis=-1).reshape(t, PEER_HEADS * PEER_TOPK)
    h_all = jnp.einsum('td,nd->tn', xf, u)
    h_sel = jnp.take_along_axis(h_all, expert, axis=1).astype(jnp.float32)
    coeff = gate * jax.nn.gelu(h_sel, approximate=False)
    mix = jnp.zeros((t, N_EXPERTS), jnp.float32).at[jnp.arange(t)[:, None], expert].add(coeff)
    return (mix.astype(x.dtype) @ v).reshape(b, s, d)


def trunk(x, norm_mix, w_in, ret_decay_fwd, ret_decay_bwd, w_branch_attn, w_branch_ret,
          w_gate, b_gate, w_out, norm_ffn, peer_w_q, peer_sub_keys, peer_u, peer_v, final_norm):
    b, s, d = x.shape
    for l in range(DEPTH):
        xn = rmsnorm(x, norm_mix[l])
        proj = xn @ w_in[l]
        att = proj[..., :ATT_COLS].reshape(b, s, 3, N_ATT_GROUPS, ATT_HEADS, ATT_HEAD_DIM)
        o_a = dilated_attention_mixer(att[:, :, 0], att[:, :, 1], att[:, :, 2])
        ret = proj[..., ATT_COLS:]
        qr = ret[..., :RET_QK_WIDTH].reshape(b, s, RET_HEADS, RET_QK_DIM)
        kr = ret[..., RET_QK_WIDTH:2 * RET_QK_WIDTH].reshape(b, s, RET_HEADS, RET_QK_DIM)
        vr = ret[..., 2 * RET_QK_WIDTH:2 * RET_QK_WIDTH + RET_V_WIDTH].reshape(b, s, RET_HEADS, RET_V_DIM)
        gr = ret[..., 2 * RET_QK_WIDTH + RET_V_WIDTH:]
        o_b = retention_mixer(qr, kr, vr, gr, ret_decay_fwd[l], ret_decay_bwd[l])
        gates = jax.nn.sigmoid((xn @ w_gate[l] + b_gate[l]).astype(jnp.float32))
        merged = (gates[..., :d] * (o_a @ w_branch_attn[l]).astype(jnp.float32)
                  + gates[..., d:] * (o_b @ w_branch_ret[l]).astype(jnp.float32))
        x = x + merged.astype(x.dtype) @ w_out[l]
        x = x + peer_ffn(rmsnorm(x, norm_ffn[l]), peer_w_q[l], peer_sub_keys[l], peer_u[l], peer_v[l])
    return rmsnorm(x, final_norm)


def setup_inputs(seed: int = 0) -> dict:
    key = jax.random.key(seed)
    ks = jax.random.split(key, 20)
    f32 = jnp.float32
    D = D_MODEL
    base_decay = jnp.log(2.0 ** (5.0 + jnp.arange(RET_HEADS, dtype=f32)) - 1.0)
    return {
        "x_prompt": jax.random.normal(ks[0], (BATCH, SEQ, D), f32),
        "x_sample": jax.random.normal(ks[1], (DEC_BATCH, DEC_SEQ, D), f32),
        "norm_mix": 1.0 + 0.01 * jax.random.normal(ks[2], (DEPTH, D), f32),
        "w_in": jax.random.normal(ks[3], (DEPTH, D, IN_COLS), f32) * D ** -0.5,
        "ret_decay_fwd": base_decay[None, :] + 0.1 * jax.random.normal(ks[4], (DEPTH, RET_HEADS), f32),
        "ret_decay_bwd": base_decay[None, :] + 0.1 * jax.random.normal(ks[5], (DEPTH, RET_HEADS), f32),
        "w_branch_attn": jax.random.normal(ks[6], (DEPTH, ATT_GROUP_WIDTH, D), f32) * ATT_GROUP_WIDTH ** -0.5,
        "w_branch_ret": jax.random.normal(ks[7], (DEPTH, RET_V_WIDTH, D), f32) * RET_V_WIDTH ** -0.5,
        "w_gate": jax.random.normal(ks[8], (DEPTH, D, 2 * D), f32) * D ** -0.5,
        "b_gate": 0.01 * jax.random.normal(ks[9], (DEPTH, 2 * D), f32),
        "w_out": jax.random.normal(ks[10], (DEPTH, D, D), f32) * D ** -0.5,
        "norm_ffn": 1.0 + 0.01 * jax.random.normal(ks[11], (DEPTH, D), f32),
        "peer_w_q": jax.random.normal(ks[12], (DEPTH, D, PEER_HEADS * 2 * PEER_KEY_DIM), f32) * D ** -0.5,
        "peer_sub_keys": jax.random.normal(ks[13], (DEPTH, PEER_HEADS, 2, N_KEYS, PEER_KEY_DIM), f32) * PEER_KEY_DIM ** -0.5,
        "peer_u": jax.random.normal(ks[14], (DEPTH, N_EXPERTS, D), f32) * D ** -0.5,
        "peer_v": jax.random.normal(ks[15], (DEPTH, N_EXPERTS, D), f32) * (PEER_HEADS * PEER_TOPK) ** -0.5,
        "final_norm": 1.0 + 0.01 * jax.random.normal(ks[16], (D,), f32),
    }


def reference(x_prompt, x_sample, norm_mix, w_in, ret_decay_fwd, ret_decay_bwd, w_branch_attn, w_branch_ret,
              w_gate, b_gate, w_out, norm_ffn, peer_w_q, peer_sub_keys, peer_u, peer_v, final_norm):
    y_prompt = trunk(x_prompt, norm_mix, w_in, ret_decay_fwd, ret_decay_bwd, w_branch_attn, w_branch_ret,
                     w_gate, b_gate, w_out, norm_ffn, peer_w_q, peer_sub_keys, peer_u, peer_v, final_norm)
    y_sample = trunk(x_sample, norm_mix, w_in, ret_decay_fwd, ret_decay_bwd, w_branch_attn, w_branch_ret,
                     w_gate, b_gate, w_out, norm_ffn, peer_w_q, peer_sub_keys, peer_u, peer_v, final_norm)
    return (y_prompt, y_sample)
```

```python
import functools
import math

import jax
import jax.numpy as jnp
from jax import lax
from jax.experimental import pallas as pl
from jax.experimental.pallas import tpu as pltpu

HEAD_DIM = 128
ATT_HEADS = 8
ATT_DILATIONS = (1, 4, 16)
ATT_HALF_WINDOW = 64
ATT_GROUP_WIDTH = ATT_HEADS * HEAD_DIM
ATT_PART = len(ATT_DILATIONS) * ATT_GROUP_WIDTH
ATT_COLS = 3 * ATT_PART
RET_HEADS = 8
RET_V_DIM = 256
RET_CHUNK = 128
RET_QK_WIDTH = RET_HEADS * HEAD_DIM
RET_V_WIDTH = RET_HEADS * RET_V_DIM
RET_Q0 = ATT_COLS
RET_K0 = RET_Q0 + RET_QK_WIDTH
RET_V0 = RET_K0 + RET_QK_WIDTH
RET_G0 = RET_V0 + RET_V_WIDTH
IN_COLS = RET_G0 + RET_V_WIDTH
PEER_HEADS = 8
N_KEYS = 128
PEER_TOPK = 16
ROPE_THETA = 10000.0
RET_THETA = 10000.0
EPS = 1e-6
NEG_INF = -1e30

LANES = 128
V7X_VMEM_BYTES = 64 * 1024 * 1024
VMEM_LIMIT = 56 * 1024 * 1024

F32 = jnp.float32
BF16 = jnp.bfloat16


def _params(sem):
    return pltpu.CompilerParams(dimension_semantics=sem, vmem_limit_bytes=VMEM_LIMIT)


def _rmsnorm_kernel(x_ref, g_ref, o_ref):
    x = x_ref[...]
    y = x * lax.rsqrt(jnp.mean(x * x, axis=-1, keepdims=True) + EPS)
    o_ref[...] = (y * g_ref[...]).astype(o_ref.dtype)


def _add_rmsnorm_kernel(x_ref, d_ref, g_ref, s_ref, o_ref):
    x = x_ref[...] + d_ref[...]
    s_ref[...] = x
    y = x * lax.rsqrt(jnp.mean(x * x, axis=-1, keepdims=True) + EPS)
    o_ref[...] = (y * g_ref[...]).astype(o_ref.dtype)


def _add_rmsnorm_final_kernel(x_ref, d_ref, g_ref, o_ref):
    x = x_ref[...] + d_ref[...]
    y = x * lax.rsqrt(jnp.mean(x * x, axis=-1, keepdims=True) + EPS)
    o_ref[...] = (y * g_ref[...]).astype(o_ref.dtype)


def _row_tile(t, cap):
    r = min(cap, t)
    assert t % r == 0
    return r


def rmsnorm(x, gain, out_dtype):
    t, d = x.shape
    tr = _row_tile(t, 256)
    row = pl.BlockSpec((tr, d), lambda i: (i, 0))
    return pl.pallas_call(
        _rmsnorm_kernel, grid=(t // tr,),
        in_specs=[row, pl.BlockSpec((1, d), lambda i: (0, 0))],
        out_specs=row, out_shape=jax.ShapeDtypeStruct((t, d), out_dtype),
        compiler_params=_params(("parallel",)), name="rmsnorm")(x, gain.reshape(1, d))


def add_rmsnorm(x, delta, gain):
    t, d = x.shape
    tr = _row_tile(t, 256)
    row = pl.BlockSpec((tr, d), lambda i: (i, 0))
    return pl.pallas_call(
        _add_rmsnorm_kernel, grid=(t // tr,),
        in_specs=[row, row, pl.BlockSpec((1, d), lambda i: (0, 0))],
        out_specs=[row, row],
        out_shape=[jax.ShapeDtypeStruct((t, d), F32), jax.ShapeDtypeStruct((t, d), BF16)],
        compiler_params=_params(("parallel",)), name="add_rmsnorm")(x, delta, gain.reshape(1, d))


def add_rmsnorm_final(x, delta, gain):
    t, d = x.shape
    tr = _row_tile(t, 256)
    row = pl.BlockSpec((tr, d), lambda i: (i, 0))
    return pl.pallas_call(
        _add_rmsnorm_final_kernel, grid=(t // tr,),
        in_specs=[row, row, pl.BlockSpec((1, d), lambda i: (0, 0))],
        out_specs=row, out_shape=jax.ShapeDtypeStruct((t, d), F32),
        compiler_params=_params(("parallel",)), name="add_rmsnorm_final")(x, delta, gain.reshape(1, d))


PROJ_TN = 512
_ROPE_TILES = 2 * ATT_PART // PROJ_TN
_ATTV_END = ATT_COLS // PROJ_TN
_RETROT_END = RET_V0 // PROJ_TN
_RETK_LO = RET_K0 // PROJ_TN


def _proj_kernel(x_ref, w_ref, rc_ref, rs_ref, qc_ref, qs_ref, o_ref):
    j = pl.program_id(1)
    acc = jnp.dot(x_ref[...], w_ref[...], preferred_element_type=F32)
    nh = PROJ_TN // HEAD_DIM

    @pl.when(j < _ROPE_TILES)
    def _():
        rc = rc_ref[...]
        rs = rs_ref[...]
        for hh in range(nh):
            cs = slice(hh * HEAD_DIM, (hh + 1) * HEAD_DIM)
            xs = acc[:, cs]
            rot = pltpu.roll(xs, HEAD_DIM // 2, axis=1)
            o_ref[:, cs] = (xs * rc + rot * rs).astype(o_ref.dtype)

    @pl.when(jnp.logical_and(j >= _ATTV_END, j < _RETROT_END))
    def _():
        qc = qc_ref[...]
        qs = qs_ref[...]
        scale = jnp.where(j >= _RETK_LO, HEAD_DIM ** -0.5, 1.0).astype(F32)
        lane = lax.broadcasted_iota(jnp.int32, (acc.shape[0], HEAD_DIM), 1)
        even = (lane % 2) == 0
        for hh in range(nh):
            cs = slice(hh * HEAD_DIM, (hh + 1) * HEAD_DIM)
            xs = acc[:, cs]
            nxt = pltpu.roll(xs, HEAD_DIM - 1, axis=1)
            prv = pltpu.roll(xs, 1, axis=1)
            sw = jnp.where(even, nxt, prv)
            o_ref[:, cs] = ((xs * qc + sw * qs) * scale).astype(o_ref.dtype)

    @pl.when(jnp.logical_or(jnp.logical_and(j >= _ROPE_TILES, j < _ATTV_END), j >= _RETROT_END))
    def _():
        o_ref[...] = acc.astype(o_ref.dtype)


def _rotation_tables(s):
    pos = jnp.arange(s, dtype=F32)[:, None]
    inv = ROPE_THETA ** (-jnp.arange(0, HEAD_DIM, 2, dtype=F32) / HEAD_DIM)
    ang = pos * inv[None, :]
    rc = jnp.concatenate([jnp.cos(ang), jnp.cos(ang)], axis=-1)
    rs = jnp.concatenate([-jnp.sin(ang), jnp.sin(ang)], axis=-1)
    inv_r = 1.0 / (RET_THETA ** jnp.linspace(0.0, 1.0, HEAD_DIM // 2, dtype=F32))
    ang_r = pos * inv_r[None, :]
    qc = jnp.repeat(jnp.cos(ang_r), 2, axis=-1)
    qs = jnp.stack([-jnp.sin(ang_r), jnp.sin(ang_r)], axis=-1).reshape(s, HEAD_DIM)
    return rc, rs, qc, qs


def input_projection(xn, w_in, tables, s):
    t, d = xn.shape
    tm = _row_tile(s, 1024)
    nsb = s // tm
    tab = pl.BlockSpec((tm, HEAD_DIM), lambda i, j: (i % nsb, 0))
    return pl.pallas_call(
        _proj_kernel, grid=(t // tm, IN_COLS // PROJ_TN),
        in_specs=[pl.BlockSpec((tm, d), lambda i, j: (i, 0)),
                  pl.BlockSpec((d, PROJ_TN), lambda i, j: (0, j)),
                  tab, tab, tab, tab],
        out_specs=pl.BlockSpec((tm, PROJ_TN), lambda i, j: (i, j)),
        out_shape=jax.ShapeDtypeStruct((t, IN_COLS), BF16),
        compiler_params=_params(("parallel", "arbitrary")), name="input_projection")(xn, w_in, *tables)


def _matmul_kernel(x_ref, w_ref, o_ref):
    o_ref[...] = jnp.dot(x_ref[...], w_ref[...], preferred_element_type=F32).astype(o_ref.dtype)


def matmul_bf16(x, w):
    t, d = x.shape
    n = w.shape[1]
    tm = _row_tile(t, 1024)
    tn = min(512, n)
    return pl.pallas_call(
        _matmul_kernel, grid=(t // tm, n // tn),
        in_specs=[pl.BlockSpec((tm, d), lambda i, j: (i, 0)),
                  pl.BlockSpec((d, tn), lambda i, j: (0, j))],
        out_specs=pl.BlockSpec((tm, tn), lambda i, j: (i, j)),
        out_shape=jax.ShapeDtypeStruct((t, n), BF16),
        compiler_params=_params(("parallel", "arbitrary")), name="matmul_bf16")(x, w)


def _att_kernel(q_ref, k_ref, v_ref, o_ref, l_ref, *, sub_len, heads):
    w = ATT_HALF_WINDOW
    bq = min(128, sub_len)
    kw = min(bq + 2 * w, sub_len)
    scale = HEAD_DIM ** -0.5
    rel = (lax.broadcasted_iota(jnp.int32, (bq, kw), 1)
           - lax.broadcasted_iota(jnp.int32, (bq, kw), 0))
    for hh in range(heads):
        cs = slice(hh * HEAD_DIM, (hh + 1) * HEAD_DIM)

        def qblock(qi, carry):
            qs = pl.multiple_of(qi * bq, bq)
            ks = pl.multiple_of(jnp.clip(qs - w, 0, sub_len - kw), w)
            q = q_ref[pl.ds(qs, bq), cs]
            k = k_ref[pl.ds(ks, kw), cs]
            v = v_ref[pl.ds(ks, kw), cs]
            s = lax.dot_general(q, k, (((1,), (1,)), ((), ())), preferred_element_type=F32) * scale
            valid = jnp.abs(rel + (ks - qs)) <= w
            s = jnp.where(valid, s, NEG_INF)
            m = jnp.max(s, axis=-1, keepdims=True)
            lse = m + jnp.log(jnp.sum(jnp.exp(s - m), axis=-1, keepdims=True))
            p = jnp.exp(s - lse)
            o_ref[pl.ds(qs, bq), cs] = jnp.dot(p.astype(BF16), v, preferred_element_type=F32)
            l_ref[pl.ds(qs, bq), cs] = jnp.broadcast_to(lse, (bq, HEAD_DIM))
            return carry

        lax.fori_loop(0, sub_len // bq, qblock, 0)


def dilated_attention_group(proj, batch, s, group):
    dil = ATT_DILATIONS[group]
    sub_len = s // dil
    heads = min(ATT_HEADS, 2 * dil)
    hw = heads * HEAD_DIM
    t = batch * s
    pv = proj.reshape(batch, sub_len, dil * IN_COLS)

    def in_spec(part):
        base = (part * ATT_PART + group * ATT_GROUP_WIDTH) // hw
        return pl.BlockSpec((None, sub_len, hw),
                            lambda b, r, hb: (b, 0, r * (IN_COLS // hw) + base + hb))

    out_spec = pl.BlockSpec((None, sub_len, hw),
                            lambda b, r, hb: (b, 0, r * (ATT_GROUP_WIDTH // hw) + hb))
    shp = jax.ShapeDtypeStruct((batch, sub_len, dil * ATT_GROUP_WIDTH), F32)
    o, lse = pl.pallas_call(
        functools.partial(_att_kernel, sub_len=sub_len, heads=heads),
        grid=(batch, dil, ATT_HEADS // heads),
        in_specs=[in_spec(0), in_spec(1), in_spec(2)],
        out_specs=[out_spec, out_spec], out_shape=[shp, shp],
        compiler_params=_params(("parallel", "parallel", "parallel")),
        name=f"dilated_attention_g{group}")(pv, pv, pv)
    return o.reshape(t, ATT_GROUP_WIDTH), lse.reshape(t, ATT_GROUP_WIDTH)


def _att_merge_kernel(o0, o1, o2, l0, l1, l2, out_ref):
    a, b, c = l0[...], l1[...], l2[...]
    m = jnp.maximum(jnp.maximum(a, b), c)
    ea, eb, ec = jnp.exp(a - m), jnp.exp(b - m), jnp.exp(c - m)
    z = ea + eb + ec
    out_ref[...] = ((ea / z) * o0[...] + (eb / z) * o1[...] + (ec / z) * o2[...]).astype(out_ref.dtype)


def attention_merge(outs, lses):
    t, wdt = outs[0].shape
    tr = _row_tile(t, 512)
    row = pl.BlockSpec((tr, wdt), lambda i: (i, 0))
    return pl.pallas_call(
        _att_merge_kernel, grid=(t // tr,), in_specs=[row] * 6, out_specs=row,
        out_shape=jax.ShapeDtypeStruct((t, wdt), BF16),
        compiler_params=_params(("parallel",)), name="attention_merge")(*outs, *lses)


def _log_sigmoid(x):
    return jnp.minimum(x, 0.0) - jnp.log1p(jnp.exp(-jnp.abs(x)))


def _retention_kernel(dec_ref, q_ref, k_ref, v_ref, g_ref, o_ref, acc_ref, *, seq):
    c = RET_CHUNK
    nc = seq // c
    h = pl.program_id(1)
    lgf = _log_sigmoid(jnp.full((1, 1), dec_ref[0, h], F32))
    lgb = _log_sigmoid(jnp.full((1, 1), dec_ref[1, h], F32))
    row = lax.broadcasted_iota(jnp.int32, (c, RET_V_DIM), 0).astype(F32)
    row_k = lax.broadcasted_iota(jnp.int32, (c, HEAD_DIM), 0).astype(F32)
    rel = (lax.broadcasted_iota(jnp.int32, (c, c), 0) - lax.broadcasted_iota(jnp.int32, (c, c), 1)).astype(F32)
    decay = jnp.where(rel >= 0, jnp.exp(jnp.maximum(rel, 0.0) * lgf),
                      jnp.exp(jnp.maximum(-rel, 0.0) * lgb))
    qdec_f = jnp.exp((row + 1.0) * lgf)
    qdec_b = jnp.exp((c - row) * lgb)
    kdec_f = jnp.exp((c - 1.0 - row_k) * lgf)
    kdec_b = jnp.exp(row_k * lgb)
    chunk_f = jnp.exp(c * lgf)
    chunk_b = jnp.exp(c * lgb)
    trans_b = (((1,), (1,)), ((), ()))

    def load(ci):
        r0 = pl.multiple_of(ci * c, c)
        return r0, q_ref[pl.ds(r0, c), :], k_ref[pl.ds(r0, c), :], v_ref[pl.ds(r0, c), :]

    def fwd(ci, state):
        r0, q, k, v = load(ci)
        a = lax.dot_general(q, k, trans_b, preferred_element_type=F32)
        intra = jnp.dot((a * decay).astype(BF16), v, preferred_element_type=F32)
        cross = jnp.dot(q, state.astype(BF16), preferred_element_type=F32) * qdec_f
        acc_ref[pl.ds(r0, c), :] = intra + cross
        kd = (k.astype(F32) * kdec_f).T.astype(BF16)
        return state * chunk_f + jnp.dot(kd, v, preferred_element_type=F32)

    lax.fori_loop(0, nc, fwd, jnp.zeros((HEAD_DIM, RET_V_DIM), F32))

    def bwd(step, state):
        r0, q, k, v = load(nc - 1 - step)
        cross = jnp.dot(q, state.astype(BF16), preferred_element_type=F32) * qdec_b
        y = acc_ref[pl.ds(r0, c), :] + cross
        y = y * lax.rsqrt(jnp.mean(y * y, axis=-1, keepdims=True) + EPS)
        g = g_ref[pl.ds(r0, c), :].astype(F32)
        o_ref[pl.ds(r0, c), :] = (g * jax.nn.sigmoid(g) * y).astype(o_ref.dtype)
        kd = (k.astype(F32) * kdec_b).T.astype(BF16)
        return state * chunk_b + jnp.dot(kd, v, preferred_element_type=F32)

    lax.fori_loop(0, nc, bwd, jnp.zeros((HEAD_DIM, RET_V_DIM), F32))


def retention(proj, decays, batch, s):
    t = batch * s
    qk = lambda base: pl.BlockSpec((s, HEAD_DIM), lambda b, h: (b, base // HEAD_DIM + h))
    vg = lambda base: pl.BlockSpec((s, RET_V_DIM), lambda b, h: (b, base // RET_V_DIM + h))
    return pl.pallas_call(
        functools.partial(_retention_kernel, seq=s),
        grid=(batch, RET_HEADS),
        in_specs=[pl.BlockSpec(memory_space=pltpu.SMEM),
                  qk(RET_Q0), qk(RET_K0), vg(RET_V0), vg(RET_G0)],
        out_specs=pl.BlockSpec((s, RET_V_DIM), lambda b, h: (b, h)),
        out_shape=jax.ShapeDtypeStruct((t, RET_V_WIDTH), BF16),
        scratch_shapes=[pltpu.VMEM((s, RET_V_DIM), F32)],
        compiler_params=_params(("parallel", "parallel")), name="retention")(decays, proj, proj, proj, proj)


def _branch_merge_kernel(xn_ref, oa_ref, ob_ref, wga_ref, wgr_ref, wa_ref, wr_ref, ba_ref, br_ref, o_ref):
    xn = xn_ref[...]
    ga = jax.nn.sigmoid(jnp.dot(xn, wga_ref[...], preferred_element_type=F32) + ba_ref[...])
    gr = jax.nn.sigmoid(jnp.dot(xn, wgr_ref[...], preferred_element_type=F32) + br_ref[...])
    a = jnp.dot(oa_ref[...], wa_ref[...], preferred_element_type=F32)
    r = jnp.dot(ob_ref[...], wr_ref[...], preferred_element_type=F32)
    o_ref[...] = (ga * a + gr * r).astype(o_ref.dtype)


def branch_merge(xn, o_a, o_b, w_gate, b_gate, w_a, w_r):
    t, d = xn.shape
    tm = _row_tile(t, 512)
    tn = min(512, d)
    nb = d // tn
    res = lambda width: pl.BlockSpec((tm, width), lambda i, n: (i, 0))
    return pl.pallas_call(
        _branch_merge_kernel, grid=(t // tm, nb),
        in_specs=[res(d), res(o_a.shape[1]), res(o_b.shape[1]),
                  pl.BlockSpec((d, tn), lambda i, n: (0, n)),
                  pl.BlockSpec((d, tn), lambda i, n: (0, nb + n)),
                  pl.BlockSpec((o_a.shape[1], tn), lambda i, n: (0, n)),
                  pl.BlockSpec((o_b.shape[1], tn), lambda i, n: (0, n)),
                  pl.BlockSpec((1, tn), lambda i, n: (0, n)),
                  pl.BlockSpec((1, tn), lambda i, n: (0, nb + n))],
        out_specs=pl.BlockSpec((tm, tn), lambda i, n: (i, n)),
        out_shape=jax.ShapeDtypeStruct((t, d), BF16),
        compiler_params=_params(("parallel", "arbitrary")), name="branch_merge")(
            xn, o_a, o_b, w_gate, w_gate, w_a, w_r, b_gate, b_gate)


def _residual_matmul_kernel(x_ref, m_ref, w_ref, o_ref):
    o_ref[...] = x_ref[...] + jnp.dot(m_ref[...], w_ref[...], preferred_element_type=F32)


def residual_matmul(x, merged, w_out):
    t, d = x.shape
    tm = _row_tile(t, 1024)
    tn = min(512, d)
    return pl.pallas_call(
        _residual_matmul_kernel, grid=(t // tm, d // tn),
        in_specs=[pl.BlockSpec((tm, tn), lambda i, n: (i, n)),
                  pl.BlockSpec((tm, d), lambda i, n: (i, 0)),
                  pl.BlockSpec((d, tn), lambda i, n: (0, n))],
        out_specs=pl.BlockSpec((tm, tn), lambda i, n: (i, n)),
        out_shape=jax.ShapeDtypeStruct((t, d), F32),
        compiler_params=_params(("parallel", "arbitrary")), name="residual_matmul")(x, merged, w_out)


PEER_SEL_TM = 128
PEER_W_ROWS = 8


def _topk_rows(x, k):
    n = x.shape[0]
    iota = lax.broadcasted_iota(jnp.int32, x.shape, 0)
    vals, idxs = [], []
    for _ in range(k):
        m = jnp.max(x, axis=0, keepdims=True)
        idx = jnp.min(jnp.where(x == m, iota, n), axis=0, keepdims=True)
        vals.append(m)
        idxs.append(idx)
        x = jnp.where(iota == idx, -jnp.inf, x)
    return jnp.concatenate(vals, axis=0), jnp.concatenate(idxs, axis=0)


def _peer_select_kernel(q_ref, keys_ref, w_ref, i_scr, j_scr, ghi_scr, glo_scr):
    tm = q_ref.shape[0]
    k = PEER_TOPK
    trans_b = (((1,), (1,)), ((), ()))
    i_rows, j_rows, g_rows = [], [], []
    for h in range(PEER_HEADS):
        tops = []
        for p in range(2):
            col = (h * 2 + p) * HEAD_DIM
            sc = lax.dot_general(keys_ref[h, p], q_ref[:, col:col + HEAD_DIM], trans_b,
                                 preferred_element_type=F32)
            tops.append(_topk_rows(sc, k))
        (v0, i0), (v1, i1) = tops
        cand = jnp.concatenate([v0[a:a + 1, :] + v1 for a in range(k)], axis=0)
        sel_s, sel_pos = _topk_rows(cand, k)
        a_sel = sel_pos // k
        b_sel = sel_pos % k
        i_sel = jnp.zeros((k, tm), jnp.int32)
        j_sel = jnp.zeros((k, tm), jnp.int32)
        for r in range(k):
            i_sel = i_sel + jnp.where(a_sel == r, i0[r:r + 1, :], 0)
            j_sel = j_sel + jnp.where(b_sel == r, i1[r:r + 1, :], 0)
        e = jnp.exp(sel_s - sel_s[0:1, :])
        g_rows.append(e / jnp.sum(e, axis=0, keepdims=True))
        i_rows.append(i_sel)
        j_rows.append(j_sel)
    g_all = jnp.concatenate(g_rows, axis=0).T
    g_hi = g_all.astype(BF16).astype(F32)
    i_scr[...] = jnp.concatenate(i_rows, axis=0).T
    j_scr[...] = jnp.concatenate(j_rows, axis=0).T
    ghi_scr[...] = g_hi
    glo_scr[...] = g_all - g_hi

    sub = lax.broadcasted_iota(jnp.int32, (N_KEYS, PEER_HEADS * k), 0)

    def token(tk, carry):
        eq_i = sub == i_scr[pl.ds(tk, 1), :]
        eq_j = sub == j_scr[pl.ds(tk, 1), :]
        lhs = jnp.concatenate([jnp.where(eq_i, ghi_scr[pl.ds(tk, 1), :], 0.0),
                               jnp.where(eq_i, glo_scr[pl.ds(tk, 1), :], 0.0)], axis=1).astype(BF16)
        one = jnp.where(eq_j, 1.0, 0.0).astype(BF16)
        rhs = jnp.concatenate([one, one], axis=1)
        wt = lax.dot_general(lhs, rhs, trans_b, preferred_element_type=F32)
        w_ref[:, tk] = wt.reshape(N_KEYS // PEER_W_ROWS, PEER_W_ROWS, N_KEYS)
        return carry

    lax.fori_loop(0, tm, token, 0)


def peer_select(q, sub_keys):
    t = q.shape[0]
    tm = _row_tile(t, PEER_SEL_TM)
    nsel = PEER_HEADS * PEER_TOPK
    nib = N_KEYS // PEER_W_ROWS
    return pl.pallas_call(
        _peer_select_kernel, grid=(t // tm,),
        in_specs=[pl.BlockSpec((tm, q.shape[1]), lambda i: (i, 0)),
                  pl.BlockSpec(sub_keys.shape, lambda i: (0, 0, 0, 0))],
        out_specs=pl.BlockSpec((nib, tm, PEER_W_ROWS, N_KEYS), lambda i: (0, i, 0, 0)),
        out_shape=jax.ShapeDtypeStruct((nib, t, PEER_W_ROWS, N_KEYS), F32),
        scratch_shapes=[pltpu.VMEM((tm, nsel), jnp.int32), pltpu.VMEM((tm, nsel), jnp.int32),
                        pltpu.VMEM((tm, nsel), F32), pltpu.VMEM((tm, nsel), F32)],
        compiler_params=_params(("parallel",)), name="peer_select")(q, sub_keys)


PEER_TE = 512


def _gelu(x):
    return 0.5 * x * (1.0 + lax.erf(x * math.sqrt(0.5)))


def _peer_experts_kernel(xn_ref, u_ref, v_ref, w_ref, o_ref):
    e = pl.program_id(1)
    tm = xn_ref.shape[0]
    sub_blocks = PEER_TE // N_KEYS
    h = lax.dot_general(xn_ref[...], u_ref[...], (((1,), (1,)), ((), ())), preferred_element_type=F32)
    base = (e % (PEER_W_ROWS // sub_blocks)) * sub_blocks
    parts = []
    for ii in range(sub_blocks):
        wsub = w_ref[pl.ds(base + ii, tm, stride=PEER_W_ROWS), :]
        parts.append((wsub * _gelu(h[:, ii * N_KEYS:(ii + 1) * N_KEYS])).astype(BF16))
    mix = jnp.concatenate(parts, axis=1)
    contrib = jnp.dot(mix, v_ref[...], preferred_element_type=F32)

    @pl.when(e == 0)
    def _():
        o_ref[...] = contrib

    @pl.when(e != 0)
    def _():
        o_ref[...] += contrib


def peer_experts(xn, u, v, w):
    t, d = xn.shape
    n_exp = u.shape[0]
    tm = _row_tile(t, 512)
    per_w = PEER_W_ROWS * N_KEYS // PEER_TE
    nt = t // tm
    w2 = w.reshape(-1, N_KEYS)
    return pl.pallas_call(
        _peer_experts_kernel, grid=(nt, n_exp // PEER_TE),
        in_specs=[pl.BlockSpec((tm, d), lambda i, e: (i, 0)),
                  pl.BlockSpec((PEER_TE, d), lambda i, e: (e, 0)),
                  pl.BlockSpec((PEER_TE, d), lambda i, e: (e, 0)),
                  pl.BlockSpec((tm * PEER_W_ROWS, N_KEYS), lambda i, e: ((e // per_w) * nt + i, 0))],
        out_specs=pl.BlockSpec((tm, d), lambda i, e: (i, 0)),
        out_shape=jax.ShapeDtypeStruct((t, d), F32),
        compiler_params=_params(("parallel", "arbitrary")), name="peer_experts")(xn, u, v, w2)


def kernel(x_prompt, x_sample, norm_mix, w_in, ret_decay_fwd, ret_decay_bwd, w_branch_attn, w_branch_ret,
           w_gate, b_gate, w_out, norm_ffn, peer_w_q, peer_sub_keys, peer_u, peer_v, final_norm):
    assert x_prompt.shape[1:] == x_sample.shape[1:]
    bp, s, d = x_prompt.shape
    batch = bp + x_sample.shape[0]
    t = batch * s
    depth = w_in.shape[0]
    assert w_in.shape[2] == IN_COLS and s % (ATT_DILATIONS[-1] * 2 * ATT_HALF_WINDOW) == 0

    x = jnp.concatenate([x_prompt, x_sample], axis=0).reshape(t, d)
    tables = _rotation_tables(s)
    delta = None
    for l in range(depth):
        if delta is None:
            xn = rmsnorm(x, norm_mix[l], BF16)
        else:
            x, xn = add_rmsnorm(x, delta, norm_mix[l])
        proj = input_projection(xn, w_in[l].astype(BF16), tables, s)
        outs, lses = zip(*[dilated_attention_group(proj, batch, s, g) for g in range(len(ATT_DILATIONS))])
        o_a = attention_merge(outs, lses)
        decays = jnp.stack([ret_decay_fwd[l], ret_decay_bwd[l]]).astype(F32)
        o_b = retention(proj, decays, batch, s)
        merged = branch_merge(xn, o_a, o_b, w_gate[l].astype(BF16), b_gate[l].reshape(1, -1).astype(F32),
                              w_branch_attn[l].astype(BF16), w_branch_ret[l].astype(BF16))
        x = residual_matmul(x, merged, w_out[l].astype(BF16))
        xn2 = rmsnorm(x, norm_ffn[l], BF16)
        q = matmul_bf16(xn2, peer_w_q[l].astype(BF16))
        w = peer_select(q, peer_sub_keys[l].astype(BF16))
        delta = peer_experts(xn2, peer_u[l].astype(BF16), peer_v[l].astype(BF16), w)
    y = add_rmsnorm_final(x, delta, final_norm).reshape(batch, s, d)
    return y[:bp], y[bp:]
```

```python
import functools
import math

import jax
import jax.numpy as jnp
from jax import lax
from jax.experimental import pallas as pl
from jax.experimental.pallas import tpu as pltpu

HEAD_DIM = 128
ATT_HEADS = 8
ATT_DILATIONS = (1, 4, 16)
ATT_HALF_WINDOW = 64
ATT_GROUP_WIDTH = ATT_HEADS * HEAD_DIM
ATT_PART = len(ATT_DILATIONS) * ATT_GROUP_WIDTH
ATT_COLS = 3 * ATT_PART
RET_HEADS = 8
RET_V_DIM = 256
RET_CHUNK = 128
RET_QK_WIDTH = RET_HEADS * HEAD_DIM
RET_V_WIDTH = RET_HEADS * RET_V_DIM
RET_COLS = 2 * RET_QK_WIDTH + 2 * RET_V_WIDTH
IN_COLS = ATT_COLS + RET_COLS
PEER_HEADS = 8
N_KEYS = 128
PEER_TOPK = 16
ROPE_THETA = 10000.0
RET_THETA = 10000.0
EPS = 1e-6
NEG_INF = -1e30

VMEM_LIMIT = 56 * 1024 * 1024

F32 = jnp.float32
BF16 = jnp.bfloat16
TRANS_B = (((1,), (1,)), ((), ()))


def _params(sem):
    return pltpu.CompilerParams(dimension_semantics=sem, vmem_limit_bytes=VMEM_LIMIT)


def _row_tile(t, cap):
    r = min(cap, t)
    assert t % r == 0
    return r


def _rms(x, g):
    return x * lax.rsqrt(jnp.mean(x * x, axis=-1, keepdims=True) + EPS) * g


def _rmsnorm_kernel(x_ref, g_ref, o_ref):
    o_ref[...] = _rms(x_ref[...], g_ref[...]).astype(o_ref.dtype)


def _add_rmsnorm_kernel(x_ref, d_ref, g_ref, s_ref, o_ref):
    x = x_ref[...] + d_ref[...]
    s_ref[...] = x
    o_ref[...] = _rms(x, g_ref[...]).astype(o_ref.dtype)


def _add_rmsnorm_split_kernel(x_ref, d_ref, g_ref, a_ref, b_ref, *, n_first):
    y = _rms(x_ref[...] + d_ref[...], g_ref[...])
    i = pl.program_id(0)

    @pl.when(i < n_first)
    def _():
        a_ref[...] = y

    @pl.when(i >= n_first)
    def _():
        b_ref[...] = y


def rmsnorm(x, gain, out_dtype):
    t, d = x.shape
    tr = _row_tile(t, 256)
    row = pl.BlockSpec((tr, d), lambda i: (i, 0))
    return pl.pallas_call(
        _rmsnorm_kernel, grid=(t // tr,),
        in_specs=[row, pl.BlockSpec((1, d), lambda i: (0, 0))],
        out_specs=row, out_shape=jax.ShapeDtypeStruct((t, d), out_dtype),
        compiler_params=_params(("parallel",)), name="rmsnorm")(x, gain.reshape(1, d))


def add_rmsnorm(x, delta, gain):
    t, d = x.shape
    tr = _row_tile(t, 256)
    row = pl.BlockSpec((tr, d), lambda i: (i, 0))
    return pl.pallas_call(
        _add_rmsnorm_kernel, grid=(t // tr,),
        in_specs=[row, row, pl.BlockSpec((1, d), lambda i: (0, 0))],
        out_specs=[row, row],
        out_shape=[jax.ShapeDtypeStruct((t, d), F32), jax.ShapeDtypeStruct((t, d), BF16)],
        compiler_params=_params(("parallel",)), name="add_rmsnorm")(x, delta, gain.reshape(1, d))


def add_rmsnorm_split(x, delta, gain, t_first):
    t, d = x.shape
    tr = _row_tile(math.gcd(t_first, t - t_first), 256)
    nf = t_first // tr
    row = pl.BlockSpec((tr, d), lambda i: (i, 0))
    return pl.pallas_call(
        functools.partial(_add_rmsnorm_split_kernel, n_first=nf), grid=(t // tr,),
        in_specs=[row, row, pl.BlockSpec((1, d), lambda i: (0, 0))],
        out_specs=[pl.BlockSpec((tr, d), lambda i: (jnp.minimum(i, nf - 1), 0)),
                   pl.BlockSpec((tr, d), lambda i: (jnp.maximum(i - nf, 0), 0))],
        out_shape=[jax.ShapeDtypeStruct((t_first, d), F32), jax.ShapeDtypeStruct((t - t_first, d), F32)],
        compiler_params=_params(("arbitrary",)), name="add_rmsnorm_split")(x, delta, gain.reshape(1, d))


PROJ_TN = 1024
_PROJ_HEADS = PROJ_TN // HEAD_DIM
_ROPE_TILES = 2 * ATT_PART // PROJ_TN
_RETROT_TILES = 2 * RET_QK_WIDTH // PROJ_TN
_RETK_TILE = RET_QK_WIDTH // PROJ_TN


def _proj_att_kernel(x_ref, w_ref, cos_ref, sin_ref, o_ref):
    j = pl.program_id(1)
    acc = jnp.dot(x_ref[...], w_ref[...], preferred_element_type=F32)

    @pl.when(j < _ROPE_TILES)
    def _():
        cos = cos_ref[...]
        sin = sin_ref[...]
        for hh in range(_PROJ_HEADS):
            cs = slice(hh * HEAD_DIM, (hh + 1) * HEAD_DIM)
            xs = acc[:, cs]
            o_ref[:, cs] = (xs * cos + pltpu.roll(xs, HEAD_DIM // 2, axis=1) * sin).astype(o_ref.dtype)

    @pl.when(j >= _ROPE_TILES)
    def _():
        o_ref[...] = acc.astype(o_ref.dtype)


def _proj_ret_kernel(x_ref, w_ref, cos_ref, sin_ref, o_ref):
    j = pl.program_id(1)
    acc = jnp.dot(x_ref[...], w_ref[...], preferred_element_type=F32)

    @pl.when(j < _RETROT_TILES)
    def _():
        cos = cos_ref[...]
        sin = sin_ref[...]
        scale = jnp.where(j >= _RETK_TILE, HEAD_DIM ** -0.5, 1.0).astype(F32)
        lane = lax.broadcasted_iota(jnp.int32, (acc.shape[0], HEAD_DIM), 1)
        even = (lane % 2) == 0
        for hh in range(_PROJ_HEADS):
            cs = slice(hh * HEAD_DIM, (hh + 1) * HEAD_DIM)
            xs = acc[:, cs]
            sw = jnp.where(even, pltpu.roll(xs, HEAD_DIM - 1, axis=1), pltpu.roll(xs, 1, axis=1))
            o_ref[:, cs] = ((xs * cos + sw * sin) * scale).astype(o_ref.dtype)

    @pl.when(j >= _RETROT_TILES)
    def _():
        o_ref[...] = acc.astype(o_ref.dtype)


def _rotation_tables(s):
    pos = jnp.arange(s, dtype=F32)[:, None]
    inv = ROPE_THETA ** (-jnp.arange(0, HEAD_DIM, 2, dtype=F32) / HEAD_DIM)
    ang = pos * inv[None, :]
    rc = jnp.concatenate([jnp.cos(ang), jnp.cos(ang)], axis=-1)
    rs = jnp.concatenate([-jnp.sin(ang), jnp.sin(ang)], axis=-1)
    inv_r = 1.0 / (RET_THETA ** jnp.linspace(0.0, 1.0, HEAD_DIM // 2, dtype=F32))
    ang_r = pos * inv_r[None, :]
    qc = jnp.repeat(jnp.cos(ang_r), 2, axis=-1)
    qs = jnp.stack([-jnp.sin(ang_r), jnp.sin(ang_r)], axis=-1).reshape(s, HEAD_DIM)
    return rc, rs, qc, qs


def input_projection(xn, w_in, tables, s):
    t, d = xn.shape
    tm = _row_tile(s, 1024)
    nsb = s // tm
    tab = pl.BlockSpec((tm, HEAD_DIM), lambda i, j: (i % nsb, 0))
    xspec = pl.BlockSpec((tm, d), lambda i, j: (i, 0))
    ospec = pl.BlockSpec((tm, PROJ_TN), lambda i, j: (i, j))
    att = pl.pallas_call(
        _proj_att_kernel, grid=(t // tm, ATT_COLS // PROJ_TN),
        in_specs=[xspec, pl.BlockSpec((d, PROJ_TN), lambda i, j: (0, j)), tab, tab],
        out_specs=ospec, out_shape=jax.ShapeDtypeStruct((t, ATT_COLS), F32),
        compiler_params=_params(("parallel", "arbitrary")), name="proj_attention")(xn, w_in, *tables[:2])
    ret0 = ATT_COLS // PROJ_TN
    ret = pl.pallas_call(
        _proj_ret_kernel, grid=(t // tm, RET_COLS // PROJ_TN),
        in_specs=[xspec, pl.BlockSpec((d, PROJ_TN), lambda i, j: (0, ret0 + j)), tab, tab],
        out_specs=ospec, out_shape=jax.ShapeDtypeStruct((t, RET_COLS), BF16),
        compiler_params=_params(("parallel", "arbitrary")), name="proj_retention")(xn, w_in, *tables[2:])
    return att, ret


def _matmul_kernel(x_ref, w_ref, o_ref):
    o_ref[...] = jnp.dot(x_ref[...], w_ref[...], preferred_element_type=F32).astype(o_ref.dtype)


def matmul_bf16(x, w):
    t, d = x.shape
    n = w.shape[1]
    tm = _row_tile(t, 1024)
    tn = min(1024, n)
    return pl.pallas_call(
        _matmul_kernel, grid=(t // tm, n // tn),
        in_specs=[pl.BlockSpec((tm, d), lambda i, j: (i, 0)),
                  pl.BlockSpec((d, tn), lambda i, j: (0, j))],
        out_specs=pl.BlockSpec((tm, tn), lambda i, j: (i, j)),
        out_shape=jax.ShapeDtypeStruct((t, n), BF16),
        compiler_params=_params(("parallel", "arbitrary")), name="matmul_bf16")(x, w)


ATT_BLOCKS = 16
ATT_UNROLL = 4


def _rows(start, size, stride):
    return pl.ds(start, size) if stride == 1 else pl.ds(start, size, stride=stride)


def _att_kernel(q0, q1, q2, k0, k1, k2, v0, v1, v2, o_ref, og0, og1, og2, lg0, lg1, lg2, *, seq):
    w = ATT_HALF_WINDOW
    scale = HEAD_DIM ** -0.5
    groups = ((q0, k0, v0, og0, lg0), (q1, k1, v1, og1, lg1), (q2, k2, v2, og2, lg2))
    for (q_ref, k_ref, v_ref, og_ref, lg_ref), dil in zip(groups, ATT_DILATIONS):
        sub_len = seq // dil
        bq = min(128, sub_len)
        kw = min(bq + 2 * w, sub_len)
        nqb = sub_len // bq
        assert nqb * dil == ATT_BLOCKS
        rel = (lax.broadcasted_iota(jnp.int32, (bq, kw), 1)
               - lax.broadcasted_iota(jnp.int32, (bq, kw), 0))

        def block(it, q_ref=q_ref, k_ref=k_ref, v_ref=v_ref, og_ref=og_ref, lg_ref=lg_ref,
                  dil=dil, sub_len=sub_len, bq=bq, kw=kw, nqb=nqb, rel=rel):
            res = it // nqb
            qs = (it % nqb) * bq
            ks = jnp.clip(qs - w, 0, sub_len - kw)
            q = q_ref[_rows(qs * dil + res, bq, dil), :].astype(BF16)
            k = k_ref[_rows(ks * dil + res, kw, dil), :].astype(BF16)
            v = v_ref[_rows(ks * dil + res, kw, dil), :].astype(BF16)
            s = lax.dot_general(q, k, TRANS_B, preferred_element_type=F32) * scale
            s = jnp.where(jnp.abs(rel + (ks - qs)) <= w, s, NEG_INF)
            m = jnp.max(s, axis=-1, keepdims=True)
            lse = m + jnp.log(jnp.sum(jnp.exp(s - m), axis=-1, keepdims=True))
            p = jnp.exp(s - lse)
            og_ref[_rows(qs * dil + res, bq, dil), :] = jnp.dot(p.astype(BF16), v, preferred_element_type=F32)
            lg_ref[_rows(qs * dil + res, bq, dil), :] = jnp.broadcast_to(lse, (bq, HEAD_DIM))

        def blocks(it, carry, block=block):
            for u in range(ATT_UNROLL):
                block(it * ATT_UNROLL + u)
            return carry

        lax.fori_loop(0, ATT_BLOCKS // ATT_UNROLL, blocks, 0)

    mb = min(256, seq)

    def merge(c, carry):
        rows = pl.ds(pl.multiple_of(c * mb, mb), mb)
        a, b, cc = lg0[rows, :], lg1[rows, :], lg2[rows, :]
        m = jnp.maximum(jnp.maximum(a, b), cc)
        ea, eb, ec = jnp.exp(a - m), jnp.exp(b - m), jnp.exp(cc - m)
        z = ea + eb + ec
        o_ref[rows, :] = ((ea / z) * og0[rows, :] + (eb / z) * og1[rows, :]
                          + (ec / z) * og2[rows, :]).astype(o_ref.dtype)
        return carry

    lax.fori_loop(0, seq // mb, merge, 0)


def dilated_attention(att, batch, s):
    t = batch * s

    def spec(part, group):
        base = (part * ATT_PART + group * ATT_GROUP_WIDTH) // HEAD_DIM
        return pl.BlockSpec((s, HEAD_DIM), lambda b, h: (b, base + h))

    ng = len(ATT_DILATIONS)
    return pl.pallas_call(
        functools.partial(_att_kernel, seq=s), grid=(batch, ATT_HEADS),
        in_specs=[spec(p, g) for p in range(3) for g in range(ng)],
        out_specs=pl.BlockSpec((s, HEAD_DIM), lambda b, h: (b, h)),
        out_shape=jax.ShapeDtypeStruct((t, ATT_GROUP_WIDTH), BF16),
        scratch_shapes=[pltpu.VMEM((s, HEAD_DIM), F32)] * (2 * ng),
        compiler_params=_params(("parallel", "parallel")), name="dilated_attention")(*([att] * (3 * ng)))


def _log_sigmoid(x):
    return jnp.minimum(x, 0.0) - jnp.log1p(jnp.exp(-jnp.abs(x)))


def _retention_kernel(dec_ref, q_ref, k_ref, v_ref, g_ref, o_ref, acc_ref, *, seq):
    c = RET_CHUNK
    nc = seq // c
    h = pl.program_id(1)
    lgf = _log_sigmoid(jnp.full((1, 1), dec_ref[0, h], F32))
    lgb = _log_sigmoid(jnp.full((1, 1), dec_ref[1, h], F32))
    row = lax.broadcasted_iota(jnp.int32, (c, RET_V_DIM), 0).astype(F32)
    row_k = lax.broadcasted_iota(jnp.int32, (c, HEAD_DIM), 0).astype(F32)
    rel = (lax.broadcasted_iota(jnp.int32, (c, c), 0) - lax.broadcasted_iota(jnp.int32, (c, c), 1)).astype(F32)
    decay = jnp.where(rel >= 0, jnp.exp(jnp.maximum(rel, 0.0) * lgf),
                      jnp.exp(jnp.maximum(-rel, 0.0) * lgb))
    qdec_f = jnp.exp((row + 1.0) * lgf)
    qdec_b = jnp.exp((c - row) * lgb)
    kdec_f = jnp.exp((c - 1.0 - row_k) * lgf)
    kdec_b = jnp.exp(row_k * lgb)
    chunk_f = jnp.exp(c * lgf)
    chunk_b = jnp.exp(c * lgb)

    def load(ci):
        r0 = pl.multiple_of(ci * c, c)
        return r0, q_ref[pl.ds(r0, c), :], k_ref[pl.ds(r0, c), :], v_ref[pl.ds(r0, c), :]

    def fwd(ci, state):
        r0, q, k, v = load(ci)
        a = lax.dot_general(q, k, TRANS_B, preferred_element_type=F32)
        intra = jnp.dot((a * decay).astype(BF16), v, preferred_element_type=F32)
        cross = jnp.dot(q, state.astype(BF16), preferred_element_type=F32) * qdec_f
        acc_ref[pl.ds(r0, c), :] = intra + cross
        kd = (k.astype(F32) * kdec_f).T.astype(BF16)
        return state * chunk_f + jnp.dot(kd, v, preferred_element_type=F32)

    lax.fori_loop(0, nc, fwd, jnp.zeros((HEAD_DIM, RET_V_DIM), F32))

    def bwd(step, state):
        r0, q, k, v = load(nc - 1 - step)
        cross = jnp.dot(q, state.astype(BF16), preferred_element_type=F32) * qdec_b
        y = acc_ref[pl.ds(r0, c), :] + cross
        y = y * lax.rsqrt(jnp.mean(y * y, axis=-1, keepdims=True) + EPS)
        g = g_ref[pl.ds(r0, c), :].astype(F32)
        o_ref[pl.ds(r0, c), :] = (g * jax.nn.sigmoid(g) * y).astype(o_ref.dtype)
        kd = (k.astype(F32) * kdec_b).T.astype(BF16)
        return state * chunk_b + jnp.dot(kd, v, preferred_element_type=F32)

    lax.fori_loop(0, nc, bwd, jnp.zeros((HEAD_DIM, RET_V_DIM), F32))


def retention(ret, decays, batch, s):
    t = batch * s
    qk = lambda base: pl.BlockSpec((s, HEAD_DIM), lambda b, h: (b, base // HEAD_DIM + h))
    vg = lambda base: pl.BlockSpec((s, RET_V_DIM), lambda b, h: (b, base // RET_V_DIM + h))
    return pl.pallas_call(
        functools.partial(_retention_kernel, seq=s),
        grid=(batch, RET_HEADS),
        in_specs=[pl.BlockSpec(memory_space=pltpu.SMEM),
                  qk(0), qk(RET_QK_WIDTH), vg(2 * RET_QK_WIDTH), vg(2 * RET_QK_WIDTH + RET_V_WIDTH)],
        out_specs=pl.BlockSpec((s, RET_V_DIM), lambda b, h: (b, h)),
        out_shape=jax.ShapeDtypeStruct((t, RET_V_WIDTH), BF16),
        scratch_shapes=[pltpu.VMEM((s, RET_V_DIM), F32)],
        compiler_params=_params(("parallel", "parallel")), name="retention")(decays, ret, ret, ret, ret)


def _branch_merge_kernel(xn_ref, oa_ref, ob_ref, wga_ref, wgr_ref, wa_ref, wr_ref, ba_ref, br_ref, o_ref):
    xn = xn_ref[...]
    ga = jax.nn.sigmoid(jnp.dot(xn, wga_ref[...], preferred_element_type=F32) + ba_ref[...])
    gr = jax.nn.sigmoid(jnp.dot(xn, wgr_ref[...], preferred_element_type=F32) + br_ref[...])
    a = jnp.dot(oa_ref[...], wa_ref[...], preferred_element_type=F32)
    r = jnp.dot(ob_ref[...], wr_ref[...], preferred_element_type=F32)
    o_ref[...] = (ga * a + gr * r).astype(o_ref.dtype)


def branch_merge(xn, o_a, o_b, w_gate, b_gate, w_a, w_r):
    t, d = xn.shape
    tm = _row_tile(t, 512)
    tn = min(512, d)
    nb = d // tn
    res = lambda width: pl.BlockSpec((tm, width), lambda i, n: (i, 0))
    return pl.pallas_call(
        _branch_merge_kernel, grid=(t // tm, nb),
        in_specs=[res(d), res(o_a.shape[1]), res(o_b.shape[1]),
                  pl.BlockSpec((d, tn), lambda i, n: (0, n)),
                  pl.BlockSpec((d, tn), lambda i, n: (0, nb + n)),
                  pl.BlockSpec((o_a.shape[1], tn), lambda i, n: (0, n)),
                  pl.BlockSpec((o_b.shape[1], tn), lambda i, n: (0, n)),
                  pl.BlockSpec((1, tn), lambda i, n: (0, n)),
                  pl.BlockSpec((1, tn), lambda i, n: (0, nb + n))],
        out_specs=pl.BlockSpec((tm, tn), lambda i, n: (i, n)),
        out_shape=jax.ShapeDtypeStruct((t, d), BF16),
        compiler_params=_params(("parallel", "arbitrary")), name="branch_merge")(
            xn, o_a, o_b, w_gate, w_gate, w_a, w_r, b_gate, b_gate)


def _residual_matmul_kernel(x_ref, m_ref, w_ref, o_ref):
    o_ref[...] = x_ref[...] + jnp.dot(m_ref[...], w_ref[...], preferred_element_type=F32)


def residual_matmul(x, merged, w_out):
    t, d = x.shape
    tm = _row_tile(t, 1024)
    tn = min(1024, d)
    return pl.pallas_call(
        _residual_matmul_kernel, grid=(t // tm, d // tn),
        in_specs=[pl.BlockSpec((tm, tn), lambda i, n: (i, n)),
                  pl.BlockSpec((tm, d), lambda i, n: (i, 0)),
                  pl.BlockSpec((d, tn), lambda i, n: (0, n))],
        out_specs=pl.BlockSpec((tm, tn), lambda i, n: (i, n)),
        out_shape=jax.ShapeDtypeStruct((t, d), F32),
        compiler_params=_params(("parallel", "arbitrary")), name="residual_matmul")(x, merged, w_out)


PEER_SEL_TM = 128
PEER_SEL_UNROLL = 8
PEER_W_ROWS = 8
_BIG_INDEX = float(1 << 20)

_PAIR_CANDIDATES = [(a, b) for a in range(PEER_TOPK) for b in range(PEER_TOPK)
                    if (a + 1) * (b + 1) <= PEER_TOPK]


def _topk_rows(x, idx, k):
    vals, idxs = [], []
    for _ in range(k):
        m = jnp.max(x, axis=0, keepdims=True)
        sel = jnp.min(jnp.where(x == m, idx, _BIG_INDEX), axis=0, keepdims=True)
        vals.append(m)
        idxs.append(sel)
        x = jnp.where(idx == sel, -jnp.inf, x)
    return vals, idxs


def _peer_select_kernel(q_ref, keys_ref, w_ref, i_scr, j_scr, ghi_scr, glo_scr):
    tm = q_ref.shape[0]
    k = PEER_TOPK
    key_idx = lax.broadcasted_iota(jnp.int32, (N_KEYS, tm), 0).astype(F32)
    ncand = len(_PAIR_CANDIDATES)
    npad = -(-ncand // 8) * 8
    cand_row = lax.broadcasted_iota(jnp.int32, (npad, tm), 0)
    cand_idx = jnp.full((npad, tm), _BIG_INDEX, F32)
    for r, (a, b) in enumerate(_PAIR_CANDIDATES):
        cand_idx = jnp.where(cand_row == r, float(a * k + b), cand_idx)
    pad_rows = [jnp.full((1, tm), -jnp.inf, F32)] * (npad - ncand)

    i_rows, j_rows, g_rows = [], [], []
    for h in range(PEER_HEADS):
        tops = []
        for p in range(2):
            col = (h * 2 + p) * HEAD_DIM
            sc = lax.dot_general(keys_ref[h, p], q_ref[:, col:col + HEAD_DIM], TRANS_B,
                                 preferred_element_type=F32)
            tops.append(_topk_rows(sc, key_idx, k))
        (v0, i0), (v1, i1) = tops
        cand = jnp.concatenate([v0[a] + v1[b] for a, b in _PAIR_CANDIDATES] + pad_rows, axis=0)
        sel_s, sel_pos = _topk_rows(cand, cand_idx, k)
        sel_s = jnp.concatenate(sel_s, axis=0)
        sel_pos = jnp.concatenate(sel_pos, axis=0)
        a_sel = jnp.floor(sel_pos * (1.0 / k))
        b_sel = sel_pos - a_sel * k
        i_sel = jnp.zeros((k, tm), F32)
        j_sel = jnp.zeros((k, tm), F32)
        for r in range(k):
            i_sel = i_sel + jnp.where(a_sel == r, i0[r], 0.0)
            j_sel = j_sel + jnp.where(b_sel == r, i1[r], 0.0)
        e = jnp.exp(sel_s - sel_s[0:1, :])
        g_rows.append(e / jnp.sum(e, axis=0, keepdims=True))
        i_rows.append(i_sel)
        j_rows.append(j_sel)
    g_all = jnp.concatenate(g_rows, axis=0).T
    g_hi = g_all.astype(BF16).astype(F32)
    i_scr[...] = jnp.concatenate(i_rows, axis=0).T
    j_scr[...] = jnp.concatenate(j_rows, axis=0).T
    ghi_scr[...] = g_hi
    glo_scr[...] = g_all - g_hi

    sub = lax.broadcasted_iota(jnp.int32, (N_KEYS, PEER_HEADS * k), 0).astype(F32)

    def tokens(it, carry):
        for u in range(PEER_SEL_UNROLL):
            tk = it * PEER_SEL_UNROLL + u
            eq_i = sub == i_scr[pl.ds(tk, 1), :]
            eq_j = sub == j_scr[pl.ds(tk, 1), :]
            lhs = jnp.concatenate([jnp.where(eq_i, ghi_scr[pl.ds(tk, 1), :], 0.0),
                                   jnp.where(eq_i, glo_scr[pl.ds(tk, 1), :], 0.0)], axis=1).astype(BF16)
            one = jnp.where(eq_j, 1.0, 0.0).astype(BF16)
            rhs = jnp.concatenate([one, one], axis=1)
            wt = lax.dot_general(lhs, rhs, TRANS_B, preferred_element_type=F32)
            w_ref[:, tk] = wt.reshape(N_KEYS // PEER_W_ROWS, PEER_W_ROWS, N_KEYS)
        return carry

    lax.fori_loop(0, tm // PEER_SEL_UNROLL, tokens, 0)


def peer_select(q, sub_keys):
    t = q.shape[0]
    tm = _row_tile(t, PEER_SEL_TM)
    nsel = PEER_HEADS * PEER_TOPK
    nib = N_KEYS // PEER_W_ROWS
    return pl.pallas_call(
        _peer_select_kernel, grid=(t // tm,),
        in_specs=[pl.BlockSpec((tm, q.shape[1]), lambda i: (i, 0)),
                  pl.BlockSpec(sub_keys.shape, lambda i: (0, 0, 0, 0))],
        out_specs=pl.BlockSpec((nib, tm, PEER_W_ROWS, N_KEYS), lambda i: (0, i, 0, 0)),
        out_shape=jax.ShapeDtypeStruct((nib, t, PEER_W_ROWS, N_KEYS), F32),
        scratch_shapes=[pltpu.VMEM((tm, nsel), F32)] * 4,
        compiler_params=_params(("parallel",)), name="peer_select")(q, sub_keys)


PEER_TE = 512


def _gelu(x):
    return 0.5 * x * (1.0 + lax.erf(x * math.sqrt(0.5)))


def _peer_experts_kernel(xn_ref, u_ref, v_ref, w_ref, o_ref, mix_scr):
    e = pl.program_id(1)
    tm = xn_ref.shape[0]
    sub_blocks = PEER_TE // N_KEYS
    slot = e % 2

    @pl.when(e == 0)
    def _():
        o_ref[...] = jnp.zeros_like(o_ref)
        mix_scr[1] = jnp.zeros(mix_scr.shape[1:], mix_scr.dtype)

    o_ref[...] += jnp.dot(mix_scr[1 - slot], v_ref[...], preferred_element_type=F32)

    h = lax.dot_general(xn_ref[...], u_ref[...], TRANS_B, preferred_element_type=F32)
    base = (e % (PEER_W_ROWS // sub_blocks)) * sub_blocks
    parts = []
    for ii in range(sub_blocks):
        wsub = w_ref[pl.ds(base + ii, tm, stride=PEER_W_ROWS), :]
        parts.append((wsub * _gelu(h[:, ii * N_KEYS:(ii + 1) * N_KEYS])).astype(BF16))
    mix_scr[slot] = jnp.concatenate(parts, axis=1)


def peer_experts(xn, u, v, w):
    t, d = xn.shape
    ne = u.shape[0] // PEER_TE
    tm = _row_tile(t, 512)
    per_w = PEER_W_ROWS * N_KEYS // PEER_TE
    nt = t // tm
    w2 = w.reshape(-1, N_KEYS)
    cur = lambda e: jnp.minimum(e, ne - 1)
    return pl.pallas_call(
        _peer_experts_kernel, grid=(nt, ne + 1),
        in_specs=[pl.BlockSpec((tm, d), lambda i, e: (i, 0)),
                  pl.BlockSpec((PEER_TE, d), lambda i, e: (cur(e), 0)),
                  pl.BlockSpec((PEER_TE, d), lambda i, e: (jnp.maximum(e - 1, 0), 0)),
                  pl.BlockSpec((tm * PEER_W_ROWS, N_KEYS), lambda i, e: ((cur(e) // per_w) * nt + i, 0))],
        out_specs=pl.BlockSpec((tm, d), lambda i, e: (i, 0)),
        out_shape=jax.ShapeDtypeStruct((t, d), F32),
        scratch_shapes=[pltpu.VMEM((2, tm, PEER_TE), BF16)],
        compiler_params=_params(("parallel", "arbitrary")), name="peer_experts")(xn, u, v, w2)


def kernel(x_prompt, x_sample, norm_mix, w_in, ret_decay_fwd, ret_decay_bwd, w_branch_attn, w_branch_ret,
           w_gate, b_gate, w_out, norm_ffn, peer_w_q, peer_sub_keys, peer_u, peer_v, final_norm):
    assert x_prompt.shape[1:] == x_sample.shape[1:]
    bp, s, d = x_prompt.shape
    bs = x_sample.shape[0]
    batch = bp + bs
    t = batch * s
    depth = w_in.shape[0]
    assert w_in.shape[2] == IN_COLS and s % (ATT_DILATIONS[-1] * 2 * ATT_HALF_WINDOW) == 0

    x = jnp.concatenate([x_prompt, x_sample], axis=0).reshape(t, d)
    tables = _rotation_tables(s)
    delta = None
    for l in range(depth):
        if delta is None:
            xn = rmsnorm(x, norm_mix[l], BF16)
        else:
            x, xn = add_rmsnorm(x, delta, norm_mix[l])
        att, ret = input_projection(xn, w_in[l].astype(BF16), tables, s)
        o_a = dilated_attention(att, batch, s)
        decays = jnp.stack([ret_decay_fwd[l], ret_decay_bwd[l]]).astype(F32)
        o_b = retention(ret, decays, batch, s)
        merged = branch_merge(xn, o_a, o_b, w_gate[l].astype(BF16), b_gate[l].reshape(1, -1).astype(F32),
                              w_branch_attn[l].astype(BF16), w_branch_ret[l].astype(BF16))
        x = residual_matmul(x, merged, w_out[l].astype(BF16))
        xn2 = rmsnorm(x, norm_ffn[l], BF16)
        q = matmul_bf16(xn2, peer_w_q[l].astype(BF16))
        w = peer_select(q, peer_sub_keys[l].astype(BF16))
        delta = peer_experts(xn2, peer_u[l].astype(BF16), peer_v[l].astype(BF16), w)
    y_p, y_s = add_rmsnorm_split(x, delta, final_norm, bp * s)
    return y_p.reshape(bp, s, d), y_s.reshape(bs, s, d)
```

```python
import functools
import math

import jax
import jax.numpy as jnp
from jax import lax
from jax.experimental import pallas as pl
from jax.experimental.pallas import tpu as pltpu

HEAD_DIM = 128
ATT_HEADS = 8
ATT_DILATIONS = (1, 4, 16)
ATT_HALF_WINDOW = 64
ATT_GROUP_WIDTH = ATT_HEADS * HEAD_DIM
ATT_PART = len(ATT_DILATIONS) * ATT_GROUP_WIDTH
ATT_COLS = 3 * ATT_PART
RET_HEADS = 8
RET_V_DIM = 256
RET_CHUNK = 128
RET_QK_WIDTH = RET_HEADS * HEAD_DIM
RET_V_WIDTH = RET_HEADS * RET_V_DIM
RET_COLS = 2 * RET_QK_WIDTH + 2 * RET_V_WIDTH
IN_COLS = ATT_COLS + RET_COLS
PEER_HEADS = 8
N_KEYS = 128
PEER_TOPK = 16
ROPE_THETA = 10000.0
RET_THETA = 10000.0
EPS = 1e-6
NEG_INF = -1e30

VMEM_LIMIT = 56 * 1024 * 1024

F32 = jnp.float32
BF16 = jnp.bfloat16
TRANS_B = (((1,), (1,)), ((), ()))


def _params(sem):
    return pltpu.CompilerParams(dimension_semantics=sem, vmem_limit_bytes=VMEM_LIMIT)


def _row_tile(t, cap):
    r = min(cap, t)
    assert t % r == 0
    return r


def _rms(x, g):
    return x * lax.rsqrt(jnp.mean(x * x, axis=-1, keepdims=True) + EPS) * g


def _rmsnorm_kernel(x_ref, g_ref, o_ref):
    o_ref[...] = _rms(x_ref[...], g_ref[...]).astype(o_ref.dtype)


def _add_rmsnorm_kernel(x_ref, d_ref, g_ref, s_ref, o_ref):
    x = x_ref[...] + d_ref[...]
    s_ref[...] = x
    o_ref[...] = _rms(x, g_ref[...]).astype(o_ref.dtype)


def _add_rmsnorm_split_kernel(x_ref, d_ref, g_ref, a_ref, b_ref, *, n_first):
    y = _rms(x_ref[...] + d_ref[...], g_ref[...])
    i = pl.program_id(0)

    @pl.when(i < n_first)
    def _():
        a_ref[...] = y

    @pl.when(i >= n_first)
    def _():
        b_ref[...] = y


def _rmsnorm_concat_kernel(a_ref, b_ref, g_ref, x_ref, o_ref, *, n_first):
    i = pl.program_id(0)

    def emit(src_ref):
        x = src_ref[...]
        x_ref[...] = x
        o_ref[...] = _rms(x, g_ref[...]).astype(o_ref.dtype)

    pl.when(i < n_first)(lambda: emit(a_ref))
    pl.when(i >= n_first)(lambda: emit(b_ref))


def rmsnorm_concat(xa, xb, gain):
    ta, d = xa.shape
    tb = xb.shape[0]
    tr = _row_tile(math.gcd(ta, tb), 256)
    nf = ta // tr
    row = pl.BlockSpec((tr, d), lambda i: (i, 0))
    return pl.pallas_call(
        functools.partial(_rmsnorm_concat_kernel, n_first=nf), grid=((ta + tb) // tr,),
        in_specs=[pl.BlockSpec((tr, d), lambda i: (jnp.minimum(i, nf - 1), 0)),
                  pl.BlockSpec((tr, d), lambda i: (jnp.maximum(i - nf, 0), 0)),
                  pl.BlockSpec((1, d), lambda i: (0, 0))],
        out_specs=[row, row],
        out_shape=[jax.ShapeDtypeStruct((ta + tb, d), F32), jax.ShapeDtypeStruct((ta + tb, d), BF16)],
        compiler_params=_params(("arbitrary",)), name="rmsnorm_concat")(xa, xb, gain.reshape(1, d))


def _cast_kernel(x_ref, o_ref):
    o_ref[...] = x_ref[...].astype(o_ref.dtype)


def cast_bf16(w):
    cols = w.shape[-1]
    w2 = w.reshape(-1, cols)
    rows = w2.shape[0]
    tr = _row_tile(rows, 512)
    tc = max(c for c in range(128, min(cols, 4096) + 1, 128) if cols % c == 0)
    blk = pl.BlockSpec((tr, tc), lambda i, j: (i, j))
    out = pl.pallas_call(
        _cast_kernel, grid=(rows // tr, cols // tc), in_specs=[blk], out_specs=blk,
        out_shape=jax.ShapeDtypeStruct((rows, cols), BF16),
        compiler_params=_params(("parallel", "parallel")), name="cast_bf16")(w2)
    return out.reshape(w.shape)


def rmsnorm(x, gain, out_dtype):
    t, d = x.shape
    tr = _row_tile(t, 256)
    row = pl.BlockSpec((tr, d), lambda i: (i, 0))
    return pl.pallas_call(
        _rmsnorm_kernel, grid=(t // tr,),
        in_specs=[row, pl.BlockSpec((1, d), lambda i: (0, 0))],
        out_specs=row, out_shape=jax.ShapeDtypeStruct((t, d), out_dtype),
        compiler_params=_params(("parallel",)), name="rmsnorm")(x, gain.reshape(1, d))


def add_rmsnorm(x, delta, gain):
    t, d = x.shape
    tr = _row_tile(t, 256)
    row = pl.BlockSpec((tr, d), lambda i: (i, 0))
    return pl.pallas_call(
        _add_rmsnorm_kernel, grid=(t // tr,),
        in_specs=[row, row, pl.BlockSpec((1, d), lambda i: (0, 0))],
        out_specs=[row, row],
        out_shape=[jax.ShapeDtypeStruct((t, d), F32), jax.ShapeDtypeStruct((t, d), BF16)],
        compiler_params=_params(("parallel",)), name="add_rmsnorm")(x, delta, gain.reshape(1, d))


def add_rmsnorm_split(x, delta, gain, t_first):
    t, d = x.shape
    tr = _row_tile(math.gcd(t_first, t - t_first), 256)
    nf = t_first // tr
    row = pl.BlockSpec((tr, d), lambda i: (i, 0))
    return pl.pallas_call(
        functools.partial(_add_rmsnorm_split_kernel, n_first=nf), grid=(t // tr,),
        in_specs=[row, row, pl.BlockSpec((1, d), lambda i: (0, 0))],
        out_specs=[pl.BlockSpec((tr, d), lambda i: (jnp.minimum(i, nf - 1), 0)),
                   pl.BlockSpec((tr, d), lambda i: (jnp.maximum(i - nf, 0), 0))],
        out_shape=[jax.ShapeDtypeStruct((t_first, d), F32), jax.ShapeDtypeStruct((t - t_first, d), F32)],
        compiler_params=_params(("arbitrary",)), name="add_rmsnorm_split")(x, delta, gain.reshape(1, d))


PROJ_TN = 1024
_PROJ_HEADS = PROJ_TN // HEAD_DIM
_ROPE_TILES = 2 * ATT_PART // PROJ_TN
_RETROT_TILES = 2 * RET_QK_WIDTH // PROJ_TN
_RETK_TILE = RET_QK_WIDTH // PROJ_TN


def _proj_att_kernel(x_ref, w_ref, cos_ref, sin_ref, o_ref):
    j = pl.program_id(1)
    acc = jnp.dot(x_ref[...], w_ref[...], preferred_element_type=F32)

    @pl.when(j < _ROPE_TILES)
    def _():
        cos = cos_ref[...]
        sin = sin_ref[...]
        for hh in range(_PROJ_HEADS):
            cs = slice(hh * HEAD_DIM, (hh + 1) * HEAD_DIM)
            xs = acc[:, cs]
            o_ref[:, cs] = (xs * cos + pltpu.roll(xs, HEAD_DIM // 2, axis=1) * sin).astype(o_ref.dtype)

    @pl.when(j >= _ROPE_TILES)
    def _():
        o_ref[...] = acc.astype(o_ref.dtype)


def _proj_ret_kernel(x_ref, w_ref, cos_ref, sin_ref, o_ref):
    j = pl.program_id(1)
    acc = jnp.dot(x_ref[...], w_ref[...], preferred_element_type=F32)

    @pl.when(j < _RETROT_TILES)
    def _():
        cos = cos_ref[...]
        sin = sin_ref[...]
        scale = jnp.where(j >= _RETK_TILE, HEAD_DIM ** -0.5, 1.0).astype(F32)
        lane = lax.broadcasted_iota(jnp.int32, (acc.shape[0], HEAD_DIM), 1)
        even = (lane % 2) == 0
        for hh in range(_PROJ_HEADS):
            cs = slice(hh * HEAD_DIM, (hh + 1) * HEAD_DIM)
            xs = acc[:, cs]
            sw = jnp.where(even, pltpu.roll(xs, HEAD_DIM - 1, axis=1), pltpu.roll(xs, 1, axis=1))
            o_ref[:, cs] = ((xs * cos + sw * sin) * scale).astype(o_ref.dtype)

    @pl.when(j >= _RETROT_TILES)
    def _():
        o_ref[...] = acc.astype(o_ref.dtype)


def _rotation_tables(s):
    pos = jnp.arange(s, dtype=F32)[:, None]
    inv = ROPE_THETA ** (-jnp.arange(0, HEAD_DIM, 2, dtype=F32) / HEAD_DIM)
    ang = pos * inv[None, :]
    rc = jnp.concatenate([jnp.cos(ang), jnp.cos(ang)], axis=-1)
    rs = jnp.concatenate([-jnp.sin(ang), jnp.sin(ang)], axis=-1)
    inv_r = 1.0 / (RET_THETA ** jnp.linspace(0.0, 1.0, HEAD_DIM // 2, dtype=F32))
    ang_r = pos * inv_r[None, :]
    qc = jnp.repeat(jnp.cos(ang_r), 2, axis=-1)
    qs = jnp.stack([-jnp.sin(ang_r), jnp.sin(ang_r)], axis=-1).reshape(s, HEAD_DIM)
    return rc, rs, qc, qs


def input_projection(xn, w_in, layer, tables, s):
    t, d = xn.shape
    tm = _row_tile(s, 1024)
    nsb = s // tm
    tab = pl.BlockSpec((tm, HEAD_DIM), lambda i, j: (i % nsb, 0))
    xspec = pl.BlockSpec((tm, d), lambda i, j: (i, 0))
    ospec = pl.BlockSpec((tm, PROJ_TN), lambda i, j: (i, j))
    att = pl.pallas_call(
        _proj_att_kernel, grid=(t // tm, ATT_COLS // PROJ_TN),
        in_specs=[xspec, pl.BlockSpec((None, d, PROJ_TN), lambda i, j: (layer, 0, j)), tab, tab],
        out_specs=ospec, out_shape=jax.ShapeDtypeStruct((t, ATT_COLS), F32),
        compiler_params=_params(("parallel", "arbitrary")), name="proj_attention")(xn, w_in, *tables[:2])
    ret0 = ATT_COLS // PROJ_TN
    ret = pl.pallas_call(
        _proj_ret_kernel, grid=(t // tm, RET_COLS // PROJ_TN),
        in_specs=[xspec, pl.BlockSpec((None, d, PROJ_TN), lambda i, j: (layer, 0, ret0 + j)), tab, tab],
        out_specs=ospec, out_shape=jax.ShapeDtypeStruct((t, RET_COLS), BF16),
        compiler_params=_params(("parallel", "arbitrary")), name="proj_retention")(xn, w_in, *tables[2:])
    return att, ret


def _matmul_kernel(x_ref, w_ref, o_ref):
    o_ref[...] = jnp.dot(x_ref[...], w_ref[...], preferred_element_type=F32).astype(o_ref.dtype)


def matmul_bf16(x, w, layer):
    t, d = x.shape
    n = w.shape[2]
    tm = _row_tile(t, 1024)
    tn = min(1024, n)
    return pl.pallas_call(
        _matmul_kernel, grid=(t // tm, n // tn),
        in_specs=[pl.BlockSpec((tm, d), lambda i, j: (i, 0)),
                  pl.BlockSpec((None, d, tn), lambda i, j: (layer, 0, j))],
        out_specs=pl.BlockSpec((tm, tn), lambda i, j: (i, j)),
        out_shape=jax.ShapeDtypeStruct((t, n), BF16),
        compiler_params=_params(("parallel", "arbitrary")), name="matmul_bf16")(x, w)


ATT_BLOCKS = 16
ATT_UNROLL = 8


def _rows(start, size, stride):
    return pl.ds(start, size) if stride == 1 else pl.ds(start, size, stride=stride)


def _att_kernel(q0, q1, q2, k0, k1, k2, v0, v1, v2, o_ref, og0, og1, og2, lg0, lg1, lg2, *, seq):
    w = ATT_HALF_WINDOW
    scale = HEAD_DIM ** -0.5
    groups = ((q0, k0, v0, og0, lg0), (q1, k1, v1, og1, lg1), (q2, k2, v2, og2, lg2))
    for (q_ref, k_ref, v_ref, og_ref, lg_ref), dil in zip(groups, ATT_DILATIONS):
        sub_len = seq // dil
        bq = min(128, sub_len)
        kw = min(bq + 2 * w, sub_len)
        nqb = sub_len // bq
        assert nqb * dil == ATT_BLOCKS
        rel = (lax.broadcasted_iota(jnp.int32, (bq, kw), 1)
               - lax.broadcasted_iota(jnp.int32, (bq, kw), 0))

        def block(it, q_ref=q_ref, k_ref=k_ref, v_ref=v_ref, og_ref=og_ref, lg_ref=lg_ref,
                  dil=dil, sub_len=sub_len, bq=bq, kw=kw, nqb=nqb, rel=rel):
            res = it // nqb
            qs = (it % nqb) * bq
            ks = jnp.clip(qs - w, 0, sub_len - kw)
            q = q_ref[_rows(qs * dil + res, bq, dil), :].astype(BF16)
            k = k_ref[_rows(ks * dil + res, kw, dil), :].astype(BF16)
            v = v_ref[_rows(ks * dil + res, kw, dil), :].astype(BF16)
            s = lax.dot_general(q, k, TRANS_B, preferred_element_type=F32) * scale
            s = jnp.where(jnp.abs(rel + (ks - qs)) <= w, s, NEG_INF)
            m = jnp.max(s, axis=-1, keepdims=True)
            lse = m + jnp.log(jnp.sum(jnp.exp(s - m), axis=-1, keepdims=True))
            p = jnp.exp(s - lse)
            og_ref[_rows(qs * dil + res, bq, dil), :] = jnp.dot(p.astype(BF16), v, preferred_element_type=F32)
            lg_ref[_rows(qs * dil + res, bq, dil), :] = jnp.broadcast_to(lse, (bq, HEAD_DIM))

        def blocks(it, carry, block=block):
            for u in range(ATT_UNROLL):
                block(it * ATT_UNROLL + u)
            return carry

        lax.fori_loop(0, ATT_BLOCKS // ATT_UNROLL, blocks, 0)

    mb = min(256, seq)

    def merge(c, carry):
        rows = pl.ds(pl.multiple_of(c * mb, mb), mb)
        a, b, cc = lg0[rows, :], lg1[rows, :], lg2[rows, :]
        m = jnp.maximum(jnp.maximum(a, b), cc)
        ea, eb, ec = jnp.exp(a - m), jnp.exp(b - m), jnp.exp(cc - m)
        z = ea + eb + ec
        o_ref[rows, :] = ((ea / z) * og0[rows, :] + (eb / z) * og1[rows, :]
                          + (ec / z) * og2[rows, :]).astype(o_ref.dtype)
        return carry

    lax.fori_loop(0, seq // mb, merge, 0)


def dilated_attention(att, batch, s):
    t = batch * s

    def spec(part, group):
        base = (part * ATT_PART + group * ATT_GROUP_WIDTH) // HEAD_DIM
        return pl.BlockSpec((s, HEAD_DIM), lambda b, h: (b, base + h))

    ng = len(ATT_DILATIONS)
    return pl.pallas_call(
        functools.partial(_att_kernel, seq=s), grid=(batch, ATT_HEADS),
        in_specs=[spec(p, g) for p in range(3) for g in range(ng)],
        out_specs=pl.BlockSpec((s, HEAD_DIM), lambda b, h: (b, h)),
        out_shape=jax.ShapeDtypeStruct((t, ATT_GROUP_WIDTH), BF16),
        scratch_shapes=[pltpu.VMEM((s, HEAD_DIM), F32)] * (2 * ng),
        compiler_params=_params(("parallel", "parallel")), name="dilated_attention")(*([att] * (3 * ng)))


def _log_sigmoid(x):
    return jnp.minimum(x, 0.0) - jnp.log1p(jnp.exp(-jnp.abs(x)))


def _retention_kernel(dec_ref, q_ref, k_ref, v_ref, g_ref, o_ref, acc_ref, accb_ref, *, seq):
    c = RET_CHUNK
    nc = seq // c
    h = pl.program_id(1)
    lgf = _log_sigmoid(jnp.full((1, 1), dec_ref[0, h], F32))
    lgb = _log_sigmoid(jnp.full((1, 1), dec_ref[1, h], F32))
    row = lax.broadcasted_iota(jnp.int32, (c, RET_V_DIM), 0).astype(F32)
    row_k = lax.broadcasted_iota(jnp.int32, (c, HEAD_DIM), 0).astype(F32)
    rel = (lax.broadcasted_iota(jnp.int32, (c, c), 0) - lax.broadcasted_iota(jnp.int32, (c, c), 1)).astype(F32)
    decay = jnp.where(rel >= 0, jnp.exp(jnp.maximum(rel, 0.0) * lgf),
                      jnp.exp(jnp.maximum(-rel, 0.0) * lgb))
    qdec_f = jnp.exp((row + 1.0) * lgf)
    qdec_b = jnp.exp((c - row) * lgb)
    kdec_f = jnp.exp((c - 1.0 - row_k) * lgf)
    kdec_b = jnp.exp(row_k * lgb)
    chunk_f = jnp.exp(c * lgf)
    chunk_b = jnp.exp(c * lgb)

    def load(ci):
        r0 = pl.multiple_of(ci * c, c)
        return r0, q_ref[pl.ds(r0, c), :], k_ref[pl.ds(r0, c), :], v_ref[pl.ds(r0, c), :]

    def scans(step, states):
        state_f, state_b = states
        r0, q, k, v = load(step)
        a = lax.dot_general(q, k, TRANS_B, preferred_element_type=F32)
        intra = jnp.dot((a * decay).astype(BF16), v, preferred_element_type=F32)
        cross = jnp.dot(q, state_f.astype(BF16), preferred_element_type=F32) * qdec_f
        acc_ref[pl.ds(r0, c), :] = intra + cross
        kd = (k.astype(F32) * kdec_f).T.astype(BF16)
        state_f = state_f * chunk_f + jnp.dot(kd, v, preferred_element_type=F32)

        r1, q1, k1, v1 = load(nc - 1 - step)
        accb_ref[pl.ds(r1, c), :] = jnp.dot(q1, state_b.astype(BF16), preferred_element_type=F32) * qdec_b
        kd1 = (k1.astype(F32) * kdec_b).T.astype(BF16)
        state_b = state_b * chunk_b + jnp.dot(kd1, v1, preferred_element_type=F32)
        return state_f, state_b

    zero = jnp.zeros((HEAD_DIM, RET_V_DIM), F32)
    lax.fori_loop(0, nc, scans, (zero, zero))

    def finish(ci, carry):
        rows = pl.ds(pl.multiple_of(ci * c, c), c)
        y = acc_ref[rows, :] + accb_ref[rows, :]
        y = y * lax.rsqrt(jnp.mean(y * y, axis=-1, keepdims=True) + EPS)
        g = g_ref[rows, :].astype(F32)
        o_ref[rows, :] = (g * jax.nn.sigmoid(g) * y).astype(o_ref.dtype)
        return carry

    lax.fori_loop(0, nc, finish, 0)


def retention(ret, decays, batch, s):
    t = batch * s
    qk = lambda base: pl.BlockSpec((s, HEAD_DIM), lambda b, h: (b, base // HEAD_DIM + h))
    vg = lambda base: pl.BlockSpec((s, RET_V_DIM), lambda b, h: (b, base // RET_V_DIM + h))
    return pl.pallas_call(
        functools.partial(_retention_kernel, seq=s),
        grid=(batch, RET_HEADS),
        in_specs=[pl.BlockSpec(memory_space=pltpu.SMEM),
                  qk(0), qk(RET_QK_WIDTH), vg(2 * RET_QK_WIDTH), vg(2 * RET_QK_WIDTH + RET_V_WIDTH)],
        out_specs=pl.BlockSpec((s, RET_V_DIM), lambda b, h: (b, h)),
        out_shape=jax.ShapeDtypeStruct((t, RET_V_WIDTH), BF16),
        scratch_shapes=[pltpu.VMEM((s, RET_V_DIM), F32)] * 2,
        compiler_params=_params(("parallel", "parallel")), name="retention")(decays, ret, ret, ret, ret)


def _branch_merge_kernel(xn_ref, oa_ref, ob_ref, wga_ref, wgr_ref, wa_ref, wr_ref, ba_ref, br_ref, o_ref):
    xn = xn_ref[...]
    ga = jax.nn.sigmoid(jnp.dot(xn, wga_ref[...], preferred_element_type=F32) + ba_ref[...])
    gr = jax.nn.sigmoid(jnp.dot(xn, wgr_ref[...], preferred_element_type=F32) + br_ref[...])
    a = jnp.dot(oa_ref[...], wa_ref[...], preferred_element_type=F32)
    r = jnp.dot(ob_ref[...], wr_ref[...], preferred_element_type=F32)
    o_ref[...] = (ga * a + gr * r).astype(o_ref.dtype)


def branch_merge(xn, o_a, o_b, w_gate, b_gate, w_a, w_r, layer):
    t, d = xn.shape
    tm = _row_tile(t, 512)
    tn = min(512, d)
    nb = d // tn
    res = lambda width: pl.BlockSpec((tm, width), lambda i, n: (i, 0))
    return pl.pallas_call(
        _branch_merge_kernel, grid=(t // tm, nb),
        in_specs=[res(d), res(o_a.shape[1]), res(o_b.shape[1]),
                  pl.BlockSpec((None, d, tn), lambda i, n: (layer, 0, n)),
                  pl.BlockSpec((None, d, tn), lambda i, n: (layer, 0, nb + n)),
                  pl.BlockSpec((None, o_a.shape[1], tn), lambda i, n: (layer, 0, n)),
                  pl.BlockSpec((None, o_b.shape[1], tn), lambda i, n: (layer, 0, n)),
                  pl.BlockSpec((None, 1, tn), lambda i, n: (layer, 0, n)),
                  pl.BlockSpec((None, 1, tn), lambda i, n: (layer, 0, nb + n))],
        out_specs=pl.BlockSpec((tm, tn), lambda i, n: (i, n)),
        out_shape=jax.ShapeDtypeStruct((t, d), BF16),
        compiler_params=_params(("parallel", "arbitrary")), name="branch_merge")(
            xn, o_a, o_b, w_gate, w_gate, w_a, w_r, b_gate, b_gate)


def _residual_matmul_kernel(x_ref, m_ref, w_ref, o_ref):
    o_ref[...] = x_ref[...] + jnp.dot(m_ref[...], w_ref[...], preferred_element_type=F32)


def residual_matmul(x, merged, w_out, layer):
    t, d = x.shape
    tm = _row_tile(t, 1024)
    tn = min(1024, d)
    return pl.pallas_call(
        _residual_matmul_kernel, grid=(t // tm, d // tn),
        in_specs=[pl.BlockSpec((tm, tn), lambda i, n: (i, n)),
                  pl.BlockSpec((tm, d), lambda i, n: (i, 0)),
                  pl.BlockSpec((None, d, tn), lambda i, n: (layer, 0, n))],
        out_specs=pl.BlockSpec((tm, tn), lambda i, n: (i, n)),
        out_shape=jax.ShapeDtypeStruct((t, d), F32),
        compiler_params=_params(("parallel", "arbitrary")), name="residual_matmul")(x, merged, w_out)


PEER_SEL_TM = 128
PEER_SEL_UNROLL = 16
PEER_W_ROWS = 8
_BIG_INDEX = float(1 << 20)

_PAIR_CANDIDATES = [(a, b) for a in range(PEER_TOPK) for b in range(PEER_TOPK)
                    if (a + 1) * (b + 1) <= PEER_TOPK]


def _topk_rows(x, idx, k):
    vals, idxs = [], []
    for _ in range(k):
        m = jnp.max(x, axis=0, keepdims=True)
        sel = jnp.min(jnp.where(x == m, idx, _BIG_INDEX), axis=0, keepdims=True)
        vals.append(m)
        idxs.append(sel)
        x = jnp.where(idx == sel, -jnp.inf, x)
    return vals, idxs


def _peer_select_kernel(q_ref, keys_ref, w_ref, i_scr, j_scr, ghi_scr, glo_scr):
    tm = q_ref.shape[0]
    k = PEER_TOPK
    key_idx = lax.broadcasted_iota(jnp.int32, (N_KEYS, tm), 0).astype(F32)
    ncand = len(_PAIR_CANDIDATES)
    npad = -(-ncand // 8) * 8
    cand_row = lax.broadcasted_iota(jnp.int32, (npad, tm), 0)
    cand_idx = jnp.full((npad, tm), _BIG_INDEX, F32)
    for r, (a, b) in enumerate(_PAIR_CANDIDATES):
        cand_idx = jnp.where(cand_row == r, float(a * k + b), cand_idx)
    pad_rows = [jnp.full((1, tm), -jnp.inf, F32)] * (npad - ncand)

    i_rows, j_rows, g_rows = [], [], []
    for h in range(PEER_HEADS):
        tops = []
        for p in range(2):
            col = (h * 2 + p) * HEAD_DIM
            sc = lax.dot_general(keys_ref[h, p], q_ref[:, col:col + HEAD_DIM], TRANS_B,
                                 preferred_element_type=F32)
            tops.append(_topk_rows(sc, key_idx, k))
        (v0, i0), (v1, i1) = tops
        cand = jnp.concatenate([v0[a] + v1[b] for a, b in _PAIR_CANDIDATES] + pad_rows, axis=0)
        sel_s, sel_pos = _topk_rows(cand, cand_idx, k)
        sel_s = jnp.concatenate(sel_s, axis=0)
        sel_pos = jnp.concatenate(sel_pos, axis=0)
        a_sel = jnp.floor(sel_pos * (1.0 / k))
        b_sel = sel_pos - a_sel * k
        i_sel = jnp.zeros((k, tm), F32)
        j_sel = jnp.zeros((k, tm), F32)
        for r in range(k):
            i_sel = i_sel + jnp.where(a_sel == r, i0[r], 0.0)
            j_sel = j_sel + jnp.where(b_sel == r, i1[r], 0.0)
        e = jnp.exp(sel_s - sel_s[0:1, :])
        g_rows.append(e / jnp.sum(e, axis=0, keepdims=True))
        i_rows.append(i_sel)
        j_rows.append(j_sel)
    g_all = jnp.concatenate(g_rows, axis=0).T
    g_hi = g_all.astype(BF16).astype(F32)
    i_scr[...] = jnp.concatenate(i_rows, axis=0).T
    j_scr[...] = jnp.concatenate(j_rows, axis=0).T
    ghi_scr[...] = g_hi
    glo_scr[...] = g_all - g_hi

    sub = lax.broadcasted_iota(jnp.int32, (N_KEYS, PEER_HEADS * k), 0).astype(F32)

    def tokens(it, carry):
        for u in range(PEER_SEL_UNROLL):
            tk = it * PEER_SEL_UNROLL + u
            eq_i = sub == i_scr[pl.ds(tk, 1), :]
            eq_j = sub == j_scr[pl.ds(tk, 1), :]
            lhs = jnp.concatenate([jnp.where(eq_i, ghi_scr[pl.ds(tk, 1), :], 0.0),
                                   jnp.where(eq_i, glo_scr[pl.ds(tk, 1), :], 0.0)], axis=1).astype(BF16)
            one = jnp.where(eq_j, 1.0, 0.0).astype(BF16)
            rhs = jnp.concatenate([one, one], axis=1)
            wt = lax.dot_general(lhs, rhs, TRANS_B, preferred_element_type=F32)
            w_ref[:, tk] = wt.reshape(N_KEYS // PEER_W_ROWS, PEER_W_ROWS, N_KEYS)
        return carry

    lax.fori_loop(0, tm // PEER_SEL_UNROLL, tokens, 0)


def peer_select(q, sub_keys, layer):
    t = q.shape[0]
    tm = _row_tile(t, PEER_SEL_TM)
    nsel = PEER_HEADS * PEER_TOPK
    nib = N_KEYS // PEER_W_ROWS
    return pl.pallas_call(
        _peer_select_kernel, grid=(t // tm,),
        in_specs=[pl.BlockSpec((tm, q.shape[1]), lambda i: (i, 0)),
                  pl.BlockSpec((None,) + sub_keys.shape[1:], lambda i: (layer, 0, 0, 0, 0))],
        out_specs=pl.BlockSpec((nib, tm, PEER_W_ROWS, N_KEYS), lambda i: (0, i, 0, 0)),
        out_shape=jax.ShapeDtypeStruct((nib, t, PEER_W_ROWS, N_KEYS), F32),
        scratch_shapes=[pltpu.VMEM((tm, nsel), F32)] * 4,
        compiler_params=_params(("parallel",)), name="peer_select")(q, sub_keys)


PEER_TE = 512
PEER_TM = 1024


def _gelu(x):
    return 0.5 * x * (1.0 + lax.erf(x * math.sqrt(0.5)))


def _peer_experts_kernel(xn_ref, u_ref, v_ref, w_ref, o_ref, mix_scr):
    e = pl.program_id(1)
    tm = xn_ref.shape[0]
    sub_blocks = PEER_TE // N_KEYS
    slot = e % 2

    @pl.when(e == 0)
    def _():
        o_ref[...] = jnp.zeros_like(o_ref)
        mix_scr[1] = jnp.zeros(mix_scr.shape[1:], mix_scr.dtype)

    o_ref[...] += jnp.dot(mix_scr[1 - slot], v_ref[...], preferred_element_type=F32)

    h = lax.dot_general(xn_ref[...], u_ref[...], TRANS_B, preferred_element_type=F32)
    base = (e % (PEER_W_ROWS // sub_blocks)) * sub_blocks
    parts = []
    for ii in range(sub_blocks):
        wsub = w_ref[pl.ds(base + ii, tm, stride=PEER_W_ROWS), :]
        parts.append((wsub * _gelu(h[:, ii * N_KEYS:(ii + 1) * N_KEYS])).astype(BF16))
    mix_scr[slot] = jnp.concatenate(parts, axis=1)


def peer_experts(xn, u, v, w, layer):
    t, d = xn.shape
    ne = u.shape[1] // PEER_TE
    tm = _row_tile(t, PEER_TM)
    per_w = PEER_W_ROWS * N_KEYS // PEER_TE
    nt = t // tm
    w2 = w.reshape(-1, N_KEYS)
    cur = lambda e: jnp.minimum(e, ne - 1)
    once = pl.Buffered(1)
    return pl.pallas_call(
        _peer_experts_kernel, grid=(nt, ne + 1),
        in_specs=[pl.BlockSpec((tm, d), lambda i, e: (i, 0), pipeline_mode=once),
                  pl.BlockSpec((None, PEER_TE, d), lambda i, e: (layer, cur(e), 0)),
                  pl.BlockSpec((None, PEER_TE, d), lambda i, e: (layer, jnp.maximum(e - 1, 0), 0)),
                  pl.BlockSpec((tm * PEER_W_ROWS, N_KEYS), lambda i, e: ((cur(e) // per_w) * nt + i, 0))],
        out_specs=pl.BlockSpec((tm, d), lambda i, e: (i, 0), pipeline_mode=once),
        out_shape=jax.ShapeDtypeStruct((t, d), F32),
        scratch_shapes=[pltpu.VMEM((2, tm, PEER_TE), BF16)],
        compiler_params=_params(("parallel", "arbitrary")), name="peer_experts")(xn, u, v, w2)


def kernel(x_prompt, x_sample, norm_mix, w_in, ret_decay_fwd, ret_decay_bwd, w_branch_attn, w_branch_ret,
           w_gate, b_gate, w_out, norm_ffn, peer_w_q, peer_sub_keys, peer_u, peer_v, final_norm):
    assert x_prompt.shape[1:] == x_sample.shape[1:]
    bp, s, d = x_prompt.shape
    bs = x_sample.shape[0]
    batch = bp + bs
    t = batch * s
    depth = w_in.shape[0]
    assert w_in.shape[2] == IN_COLS and s % (ATT_DILATIONS[-1] * 2 * ATT_HALF_WINDOW) == 0

    w_in, w_gate, w_out, w_a, w_r, w_q, u, v = (
        cast_bf16(w) for w in (w_in, w_gate, w_out, w_branch_attn, w_branch_ret, peer_w_q, peer_u, peer_v))
    sub_keys = peer_sub_keys.astype(BF16)
    b_gate = b_gate.astype(F32).reshape(depth, 1, -1)
    tables = _rotation_tables(s)
    x = delta = None
    for l in range(depth):
        if l == 0:
            x, xn = rmsnorm_concat(x_prompt.reshape(bp * s, d), x_sample.reshape(bs * s, d), norm_mix[l])
        else:
            x, xn = add_rmsnorm(x, delta, norm_mix[l])
        att, ret = input_projection(xn, w_in, l, tables, s)
        o_a = dilated_attention(att, batch, s)
        decays = jnp.stack([ret_decay_fwd[l], ret_decay_bwd[l]]).astype(F32)
        o_b = retention(ret, decays, batch, s)
        merged = branch_merge(xn, o_a, o_b, w_gate, b_gate, w_a, w_r, l)
        x = residual_matmul(x, merged, w_out, l)
        xn2 = rmsnorm(x, norm_ffn[l], BF16)
        q = matmul_bf16(xn2, w_q, l)
        w = peer_select(q, sub_keys, l)
        delta = peer_experts(xn2, u, v, w, l)
    y_p, y_s = add_rmsnorm_split(x, delta, final_norm, bp * s)
    return y_p.reshape(bp, s, d), y_s.reshape(bs, s, d)
```

```python
import functools
import math

import jax
import jax.numpy as jnp
from jax import lax
from jax.experimental import pallas as pl
from jax.experimental.pallas import tpu as pltpu

HEAD_DIM = 128
ATT_HEADS = 8
ATT_DILATIONS = (1, 4, 16)
ATT_HALF_WINDOW = 64
ATT_GROUP_WIDTH = ATT_HEADS * HEAD_DIM
ATT_PART = len(ATT_DILATIONS) * ATT_GROUP_WIDTH
ATT_COLS = 3 * ATT_PART
RET_HEADS = 8
RET_V_DIM = 256
RET_CHUNK = 128
RET_QK_WIDTH = RET_HEADS * HEAD_DIM
RET_V_WIDTH = RET_HEADS * RET_V_DIM
RET_COLS = 2 * RET_QK_WIDTH + 2 * RET_V_WIDTH
IN_COLS = ATT_COLS + RET_COLS
PEER_HEADS = 8
N_KEYS = 128
PEER_TOPK = 16
ROPE_THETA = 10000.0
RET_THETA = 10000.0
EPS = 1e-6
NEG_INF = -1e30

VMEM_LIMIT = 56 * 1024 * 1024

F32 = jnp.float32
BF16 = jnp.bfloat16
TRANS_B = (((1,), (1,)), ((), ()))


def _params(sem):
    return pltpu.CompilerParams(dimension_semantics=sem, vmem_limit_bytes=VMEM_LIMIT)


def _row_tile(t, cap):
    r = min(cap, t)
    assert t % r == 0
    return r


def _rms(x, g):
    return x * lax.rsqrt(jnp.mean(x * x, axis=-1, keepdims=True) + EPS) * g


def _rmsnorm_kernel(x_ref, g_ref, o_ref):
    o_ref[...] = _rms(x_ref[...], g_ref[...]).astype(o_ref.dtype)


def _add_rmsnorm_kernel(x_ref, d_ref, g_ref, s_ref, o_ref):
    x = x_ref[...] + d_ref[...]
    s_ref[...] = x
    o_ref[...] = _rms(x, g_ref[...]).astype(o_ref.dtype)


def _add_rmsnorm_split_kernel(x_ref, d_ref, g_ref, a_ref, b_ref, *, n_first):
    y = _rms(x_ref[...] + d_ref[...], g_ref[...])
    i = pl.program_id(0)

    @pl.when(i < n_first)
    def _():
        a_ref[...] = y

    @pl.when(i >= n_first)
    def _():
        b_ref[...] = y


def _rmsnorm_concat_kernel(a_ref, b_ref, g_ref, x_ref, o_ref, *, n_first):
    i = pl.program_id(0)

    def emit(src_ref):
        x = src_ref[...]
        x_ref[...] = x
        o_ref[...] = _rms(x, g_ref[...]).astype(o_ref.dtype)

    pl.when(i < n_first)(lambda: emit(a_ref))
    pl.when(i >= n_first)(lambda: emit(b_ref))


def rmsnorm_concat(xa, xb, gain):
    ta, d = xa.shape
    tb = xb.shape[0]
    tr = _row_tile(math.gcd(ta, tb), 256)
    nf = ta // tr
    row = pl.BlockSpec((tr, d), lambda i: (i, 0))
    return pl.pallas_call(
        functools.partial(_rmsnorm_concat_kernel, n_first=nf), grid=((ta + tb) // tr,),
        in_specs=[pl.BlockSpec((tr, d), lambda i: (jnp.minimum(i, nf - 1), 0)),
                  pl.BlockSpec((tr, d), lambda i: (jnp.maximum(i - nf, 0), 0)),
                  pl.BlockSpec((1, d), lambda i: (0, 0))],
        out_specs=[row, row],
        out_shape=[jax.ShapeDtypeStruct((ta + tb, d), F32), jax.ShapeDtypeStruct((ta + tb, d), BF16)],
        compiler_params=_params(("arbitrary",)), name="rmsnorm_concat")(xa, xb, gain.reshape(1, d))


def _cast_kernel(x_ref, o_ref):
    o_ref[...] = x_ref[...].astype(o_ref.dtype)


def cast_bf16(w):
    cols = w.shape[-1]
    w2 = w.reshape(-1, cols)
    rows = w2.shape[0]
    tr = _row_tile(rows, 512)
    tc = max(c for c in range(128, min(cols, 4096) + 1, 128) if cols % c == 0)
    blk = pl.BlockSpec((tr, tc), lambda i, j: (i, j))
    out = pl.pallas_call(
        _cast_kernel, grid=(rows // tr, cols // tc), in_specs=[blk], out_specs=blk,
        out_shape=jax.ShapeDtypeStruct((rows, cols), BF16),
        compiler_params=_params(("parallel", "parallel")), name="cast_bf16")(w2)
    return out.reshape(w.shape)


def rmsnorm(x, gain, out_dtype):
    t, d = x.shape
    tr = _row_tile(t, 256)
    row = pl.BlockSpec((tr, d), lambda i: (i, 0))
    return pl.pallas_call(
        _rmsnorm_kernel, grid=(t // tr,),
        in_specs=[row, pl.BlockSpec((1, d), lambda i: (0, 0))],
        out_specs=row, out_shape=jax.ShapeDtypeStruct((t, d), out_dtype),
        compiler_params=_params(("parallel",)), name="rmsnorm")(x, gain.reshape(1, d))


def add_rmsnorm(x, delta, gain):
    t, d = x.shape
    tr = _row_tile(t, 256)
    row = pl.BlockSpec((tr, d), lambda i: (i, 0))
    return pl.pallas_call(
        _add_rmsnorm_kernel, grid=(t // tr,),
        in_specs=[row, row, pl.BlockSpec((1, d), lambda i: (0, 0))],
        out_specs=[row, row],
        out_shape=[jax.ShapeDtypeStruct((t, d), F32), jax.ShapeDtypeStruct((t, d), BF16)],
        compiler_params=_params(("parallel",)), name="add_rmsnorm")(x, delta, gain.reshape(1, d))


def add_rmsnorm_split(x, delta, gain, t_first):
    t, d = x.shape
    tr = _row_tile(math.gcd(t_first, t - t_first), 256)
    nf = t_first // tr
    row = pl.BlockSpec((tr, d), lambda i: (i, 0))
    return pl.pallas_call(
        functools.partial(_add_rmsnorm_split_kernel, n_first=nf), grid=(t // tr,),
        in_specs=[row, row, pl.BlockSpec((1, d), lambda i: (0, 0))],
        out_specs=[pl.BlockSpec((tr, d), lambda i: (jnp.minimum(i, nf - 1), 0)),
                   pl.BlockSpec((tr, d), lambda i: (jnp.maximum(i - nf, 0), 0))],
        out_shape=[jax.ShapeDtypeStruct((t_first, d), F32), jax.ShapeDtypeStruct((t - t_first, d), F32)],
        compiler_params=_params(("arbitrary",)), name="add_rmsnorm_split")(x, delta, gain.reshape(1, d))


PROJ_TN = 1024
PROJ_CHUNK = 256
_ROPE_TILES = 2 * ATT_PART // PROJ_TN
_RETROT_TILES = 2 * RET_QK_WIDTH // PROJ_TN
_RETK_TILE = RET_QK_WIDTH // PROJ_TN


def _proj_att_kernel(x_ref, w_ref, cos_ref, sin_ref, o_ref):
    rotate = pl.program_id(1) < _ROPE_TILES
    cos = cos_ref[...]
    sin = sin_ref[...]
    x = x_ref[...]
    for c0 in range(0, PROJ_TN, PROJ_CHUNK):
        acc = jnp.dot(x, w_ref[:, c0:c0 + PROJ_CHUNK], preferred_element_type=F32)
        for h0 in range(0, PROJ_CHUNK, HEAD_DIM):
            xs = acc[:, h0:h0 + HEAD_DIM]
            rot = xs * cos + pltpu.roll(xs, HEAD_DIM // 2, axis=1) * sin
            o_ref[:, c0 + h0:c0 + h0 + HEAD_DIM] = jnp.where(rotate, rot, xs).astype(o_ref.dtype)


def _proj_ret_kernel(x_ref, w_ref, cos_ref, sin_ref, o_ref):
    j = pl.program_id(1)
    rotate = j < _RETROT_TILES
    scale = jnp.where(j == _RETK_TILE, HEAD_DIM ** -0.5, 1.0).astype(F32)
    cos = cos_ref[...]
    sin = sin_ref[...]
    x = x_ref[...]
    even = (lax.broadcasted_iota(jnp.int32, (x.shape[0], HEAD_DIM), 1) % 2) == 0
    for c0 in range(0, PROJ_TN, PROJ_CHUNK):
        acc = jnp.dot(x, w_ref[:, c0:c0 + PROJ_CHUNK], preferred_element_type=F32)
        for h0 in range(0, PROJ_CHUNK, HEAD_DIM):
            xs = acc[:, h0:h0 + HEAD_DIM]
            sw = jnp.where(even, pltpu.roll(xs, HEAD_DIM - 1, axis=1), pltpu.roll(xs, 1, axis=1))
            rot = (xs * cos + sw * sin) * scale
            o_ref[:, c0 + h0:c0 + h0 + HEAD_DIM] = jnp.where(rotate, rot, xs).astype(o_ref.dtype)


def _rotation_tables(s):
    pos = jnp.arange(s, dtype=F32)[:, None]
    inv = ROPE_THETA ** (-jnp.arange(0, HEAD_DIM, 2, dtype=F32) / HEAD_DIM)
    ang = pos * inv[None, :]
    rc = jnp.concatenate([jnp.cos(ang), jnp.cos(ang)], axis=-1)
    rs = jnp.concatenate([-jnp.sin(ang), jnp.sin(ang)], axis=-1)
    inv_r = 1.0 / (RET_THETA ** jnp.linspace(0.0, 1.0, HEAD_DIM // 2, dtype=F32))
    ang_r = pos * inv_r[None, :]
    qc = jnp.repeat(jnp.cos(ang_r), 2, axis=-1)
    qs = jnp.stack([-jnp.sin(ang_r), jnp.sin(ang_r)], axis=-1).reshape(s, HEAD_DIM)
    return rc, rs, qc, qs


def input_projection(xn, w_in, layer, tables, s):
    t, d = xn.shape
    tm = _row_tile(s, 1024)
    nsb = s // tm
    tab = pl.BlockSpec((tm, HEAD_DIM), lambda i, j: (i % nsb, 0))
    xspec = pl.BlockSpec((tm, d), lambda i, j: (i, 0))
    ospec = pl.BlockSpec((tm, PROJ_TN), lambda i, j: (i, j))
    att = pl.pallas_call(
        _proj_att_kernel, grid=(t // tm, ATT_COLS // PROJ_TN),
        in_specs=[xspec, pl.BlockSpec((None, d, PROJ_TN), lambda i, j: (layer, 0, j)), tab, tab],
        out_specs=ospec, out_shape=jax.ShapeDtypeStruct((t, ATT_COLS), F32),
        compiler_params=_params(("parallel", "arbitrary")), name="proj_attention")(xn, w_in, *tables[:2])
    ret0 = ATT_COLS // PROJ_TN
    ret = pl.pallas_call(
        _proj_ret_kernel, grid=(t // tm, RET_COLS // PROJ_TN),
        in_specs=[xspec, pl.BlockSpec((None, d, PROJ_TN), lambda i, j: (layer, 0, ret0 + j)), tab, tab],
        out_specs=ospec, out_shape=jax.ShapeDtypeStruct((t, RET_COLS), BF16),
        compiler_params=_params(("parallel", "arbitrary")), name="proj_retention")(xn, w_in, *tables[2:])
    return att, ret


ATT_BLOCKS = 16
ATT_UNROLL = 8


def _rows(start, size, stride):
    return pl.ds(start, size) if stride == 1 else pl.ds(start, size, stride=stride)


def _att_kernel(q0, q1, q2, k0, k1, k2, v0, v1, v2, o_ref, og0, og1, og2, lg0, lg1, lg2, *, seq):
    w = ATT_HALF_WINDOW
    scale = HEAD_DIM ** -0.5
    groups = ((q0, k0, v0, og0, lg0), (q1, k1, v1, og1, lg1), (q2, k2, v2, og2, lg2))
    for (q_ref, k_ref, v_ref, og_ref, lg_ref), dil in zip(groups, ATT_DILATIONS):
        sub_len = seq // dil
        bq = min(128, sub_len)
        kw = min(bq + 2 * w, sub_len)
        nqb = sub_len // bq
        assert nqb * dil == ATT_BLOCKS
        rel = (lax.broadcasted_iota(jnp.int32, (bq, kw), 1)
               - lax.broadcasted_iota(jnp.int32, (bq, kw), 0))

        def block(it, q_ref=q_ref, k_ref=k_ref, v_ref=v_ref, og_ref=og_ref, lg_ref=lg_ref,
                  dil=dil, sub_len=sub_len, bq=bq, kw=kw, nqb=nqb, rel=rel):
            res = it // nqb
            qs = (it % nqb) * bq
            ks = jnp.clip(qs - w, 0, sub_len - kw)
            q = q_ref[_rows(qs * dil + res, bq, dil), :].astype(BF16)
            k = k_ref[_rows(ks * dil + res, kw, dil), :].astype(BF16)
            v = v_ref[_rows(ks * dil + res, kw, dil), :].astype(BF16)
            s = lax.dot_general(q, k, TRANS_B, preferred_element_type=F32) * scale
            s = jnp.where(jnp.abs(rel + (ks - qs)) <= w, s, NEG_INF)
            m = jnp.max(s, axis=-1, keepdims=True)
            lse = m + jnp.log(jnp.sum(jnp.exp(s - m), axis=-1, keepdims=True))
            p = jnp.exp(s - lse)
            og_ref[_rows(qs * dil + res, bq, dil), :] = jnp.dot(p.astype(BF16), v, preferred_element_type=F32)
            lg_ref[_rows(qs * dil + res, bq, dil), :] = jnp.broadcast_to(lse, (bq, HEAD_DIM))

        def blocks(it, carry, block=block):
            for u in range(ATT_UNROLL):
                block(it * ATT_UNROLL + u)
            return carry

        lax.fori_loop(0, ATT_BLOCKS // ATT_UNROLL, blocks, 0)

    mb = min(256, seq)

    def merge(c, carry):
        rows = pl.ds(pl.multiple_of(c * mb, mb), mb)
        a, b, cc = lg0[rows, :], lg1[rows, :], lg2[rows, :]
        m = jnp.maximum(jnp.maximum(a, b), cc)
        ea, eb, ec = jnp.exp(a - m), jnp.exp(b - m), jnp.exp(cc - m)
        z = ea + eb + ec
        o_ref[rows, :] = ((ea / z) * og0[rows, :] + (eb / z) * og1[rows, :]
                          + (ec / z) * og2[rows, :]).astype(o_ref.dtype)
        return carry

    lax.fori_loop(0, seq // mb, merge, 0)


def dilated_attention(att, batch, s):
    t = batch * s

    def spec(part, group):
        base = (part * ATT_PART + group * ATT_GROUP_WIDTH) // HEAD_DIM
        return pl.BlockSpec((s, HEAD_DIM), lambda b, h: (b, base + h))

    ng = len(ATT_DILATIONS)
    return pl.pallas_call(
        functools.partial(_att_kernel, seq=s), grid=(batch, ATT_HEADS),
        in_specs=[spec(p, g) for p in range(3) for g in range(ng)],
        out_specs=pl.BlockSpec((s, HEAD_DIM), lambda b, h: (b, h)),
        out_shape=jax.ShapeDtypeStruct((t, ATT_GROUP_WIDTH), BF16),
        scratch_shapes=[pltpu.VMEM((s, HEAD_DIM), F32)] * (2 * ng),
        compiler_params=_params(("parallel", "parallel")), name="dilated_attention")(*([att] * (3 * ng)))


def _log_sigmoid(x):
    return jnp.minimum(x, 0.0) - jnp.log1p(jnp.exp(-jnp.abs(x)))


RET_STEP_HEADS = 4


def _retention_kernel(dec_ref, q_ref, k_ref, v_ref, g_ref, o_ref, acc_ref, accb_ref, *, seq):
    c = RET_CHUNK
    nc = seq // c
    hs = RET_STEP_HEADS
    h0 = pl.program_id(1) * hs
    row = lax.broadcasted_iota(jnp.int32, (c, RET_V_DIM), 0).astype(F32)
    row_k = lax.broadcasted_iota(jnp.int32, (c, HEAD_DIM), 0).astype(F32)
    rel = (lax.broadcasted_iota(jnp.int32, (c, c), 0) - lax.broadcasted_iota(jnp.int32, (c, c), 1)).astype(F32)
    consts = []
    for hh in range(hs):
        lgf = _log_sigmoid(jnp.full((1, 1), dec_ref[0, h0 + hh], F32))
        lgb = _log_sigmoid(jnp.full((1, 1), dec_ref[1, h0 + hh], F32))
        consts.append(dict(
            decay=jnp.where(rel >= 0, jnp.exp(jnp.maximum(rel, 0.0) * lgf), jnp.exp(jnp.maximum(-rel, 0.0) * lgb)),
            qdec_f=jnp.exp((row + 1.0) * lgf),
            qdec_b=jnp.exp((c - row) * lgb),
            kdec_f=jnp.exp((c - 1.0 - row_k) * lgf),
            kdec_b=jnp.exp(row_k * lgb),
            chunk_f=jnp.exp(c * lgf), chunk_b=jnp.exp(c * lgb)))

    def load(ci, hh):
        rows = pl.ds(pl.multiple_of(ci * c, c), c)
        qk = slice(hh * HEAD_DIM, (hh + 1) * HEAD_DIM)
        vv = slice(hh * RET_V_DIM, (hh + 1) * RET_V_DIM)
        return rows, vv, q_ref[rows, qk], k_ref[rows, qk], v_ref[rows, vv]

    def scans(step, states):
        out = []
        for hh, (state_f, state_b) in enumerate(states):
            cn = consts[hh]
            rows, vv, q, k, v = load(step, hh)
            a = lax.dot_general(q, k, TRANS_B, preferred_element_type=F32)
            intra = jnp.dot((a * cn["decay"]).astype(BF16), v, preferred_element_type=F32)
            cross = jnp.dot(q, state_f.astype(BF16), preferred_element_type=F32) * cn["qdec_f"]
            acc_ref[rows, vv] = intra + cross
            kd = (k.astype(F32) * cn["kdec_f"]).T.astype(BF16)
            state_f = state_f * cn["chunk_f"] + jnp.dot(kd, v, preferred_element_type=F32)

            rows1, _, q1, k1, v1 = load(nc - 1 - step, hh)
            accb_ref[rows1, vv] = jnp.dot(q1, state_b.astype(BF16), preferred_element_type=F32) * cn["qdec_b"]
            kd1 = (k1.astype(F32) * cn["kdec_b"]).T.astype(BF16)
            state_b = state_b * cn["chunk_b"] + jnp.dot(kd1, v1, preferred_element_type=F32)
            out.append((state_f, state_b))
        return tuple(out)

    zero = jnp.zeros((HEAD_DIM, RET_V_DIM), F32)
    lax.fori_loop(0, nc, scans, ((zero, zero),) * hs)

    def finish(ci, carry):
        rows = pl.ds(pl.multiple_of(ci * c, c), c)
        for hh in range(hs):
            vv = slice(hh * RET_V_DIM, (hh + 1) * RET_V_DIM)
            y = acc_ref[rows, vv] + accb_ref[rows, vv]
            y = y * lax.rsqrt(jnp.mean(y * y, axis=-1, keepdims=True) + EPS)
            g = g_ref[rows, vv].astype(F32)
            o_ref[rows, vv] = (g * jax.nn.sigmoid(g) * y).astype(o_ref.dtype)
        return carry

    lax.fori_loop(0, nc, finish, 0)


def retention(ret, decays, batch, s):
    t = batch * s
    hs = RET_STEP_HEADS
    qw, vw = hs * HEAD_DIM, hs * RET_V_DIM
    qk = lambda base: pl.BlockSpec((s, qw), lambda b, h: (b, base // qw + h))
    vg = lambda base: pl.BlockSpec((s, vw), lambda b, h: (b, base // vw + h))
    return pl.pallas_call(
        functools.partial(_retention_kernel, seq=s),
        grid=(batch, RET_HEADS // hs),
        in_specs=[pl.BlockSpec(memory_space=pltpu.SMEM),
                  qk(0), qk(RET_QK_WIDTH), vg(2 * RET_QK_WIDTH), vg(2 * RET_QK_WIDTH + RET_V_WIDTH)],
        out_specs=pl.BlockSpec((s, vw), lambda b, h: (b, h)),
        out_shape=jax.ShapeDtypeStruct((t, RET_V_WIDTH), BF16),
        scratch_shapes=[pltpu.VMEM((s, vw), F32)] * 2,
        compiler_params=_params(("parallel", "parallel")), name="retention")(decays, ret, ret, ret, ret)


def _branch_merge_kernel(xn_ref, oa_ref, ob_ref, wga_ref, wgr_ref, wa_ref, wr_ref, ba_ref, br_ref, o_ref):
    xn = xn_ref[...]
    ga = jax.nn.sigmoid(jnp.dot(xn, wga_ref[...], preferred_element_type=F32) + ba_ref[...])
    gr = jax.nn.sigmoid(jnp.dot(xn, wgr_ref[...], preferred_element_type=F32) + br_ref[...])
    a = jnp.dot(oa_ref[...], wa_ref[...], preferred_element_type=F32)
    r = jnp.dot(ob_ref[...], wr_ref[...], preferred_element_type=F32)
    o_ref[...] = (ga * a + gr * r).astype(o_ref.dtype)


def branch_merge(xn, o_a, o_b, w_gate, b_gate, w_a, w_r, layer):
    t, d = xn.shape
    tm = _row_tile(t, 512)
    tn = min(512, d)
    nb = d // tn
    res = lambda width: pl.BlockSpec((tm, width), lambda i, n: (i, 0))
    return pl.pallas_call(
        _branch_merge_kernel, grid=(t // tm, nb),
        in_specs=[res(d), res(o_a.shape[1]), res(o_b.shape[1]),
                  pl.BlockSpec((None, d, tn), lambda i, n: (layer, 0, n)),
                  pl.BlockSpec((None, d, tn), lambda i, n: (layer, 0, nb + n)),
                  pl.BlockSpec((None, o_a.shape[1], tn), lambda i, n: (layer, 0, n)),
                  pl.BlockSpec((None, o_b.shape[1], tn), lambda i, n: (layer, 0, n)),
                  pl.BlockSpec((None, 1, tn), lambda i, n: (layer, 0, n)),
                  pl.BlockSpec((None, 1, tn), lambda i, n: (layer, 0, nb + n))],
        out_specs=pl.BlockSpec((tm, tn), lambda i, n: (i, n)),
        out_shape=jax.ShapeDtypeStruct((t, d), BF16),
        compiler_params=_params(("parallel", "arbitrary")), name="branch_merge")(
            xn, o_a, o_b, w_gate, w_gate, w_a, w_r, b_gate, b_gate)


def _residual_matmul_kernel(x_ref, m_ref, w_ref, o_ref):
    o_ref[...] = x_ref[...] + jnp.dot(m_ref[...], w_ref[...], preferred_element_type=F32)


def residual_matmul(x, merged, w_out, layer):
    t, d = x.shape
    tm = _row_tile(t, 1024)
    tn = min(1024, d)
    return pl.pallas_call(
        _residual_matmul_kernel, grid=(t // tm, d // tn),
        in_specs=[pl.BlockSpec((tm, tn), lambda i, n: (i, n)),
                  pl.BlockSpec((tm, d), lambda i, n: (i, 0)),
                  pl.BlockSpec((None, d, tn), lambda i, n: (layer, 0, n))],
        out_specs=pl.BlockSpec((tm, tn), lambda i, n: (i, n)),
        out_shape=jax.ShapeDtypeStruct((t, d), F32),
        compiler_params=_params(("parallel", "arbitrary")), name="residual_matmul")(x, merged, w_out)


PEER_SEL_TM = 128
PEER_SEL_HEADS = 4
PEER_W_ROWS = 8
_BIG_INDEX = float(1 << 20)
_NSEL = PEER_HEADS * PEER_TOPK

_PAIR_CANDIDATES = [(a, b) for a in range(PEER_TOPK) for b in range(PEER_TOPK)
                    if (a + 1) * (b + 1) <= PEER_TOPK]


def _peer_query_kernel(x_ref, w_ref, o_ref):
    acc = jnp.dot(x_ref[...], w_ref[...], preferred_element_type=F32).astype(o_ref.dtype)
    for c in range(o_ref.shape[0]):
        o_ref[c] = acc[:, c * HEAD_DIM:(c + 1) * HEAD_DIM]


def peer_query(x, w, layer):
    t, d = x.shape
    n = w.shape[2]
    tm = _row_tile(t, 1024)
    tn = min(1024, n)
    return pl.pallas_call(
        _peer_query_kernel, grid=(t // tm, n // tn),
        in_specs=[pl.BlockSpec((tm, d), lambda i, j: (i, 0)),
                  pl.BlockSpec((None, d, tn), lambda i, j: (layer, 0, j))],
        out_specs=pl.BlockSpec((tn // HEAD_DIM, tm, HEAD_DIM), lambda i, j: (j, i, 0)),
        out_shape=jax.ShapeDtypeStruct((n // HEAD_DIM, t, HEAD_DIM), BF16),
        compiler_params=_params(("parallel", "arbitrary")), name="peer_query")(x, w)


def _topk_rows(x, idx, k):
    vals, idxs = [], []
    for _ in range(k):
        m = jnp.max(x, axis=0, keepdims=True)
        sel = jnp.min(jnp.where(x == m, idx, _BIG_INDEX), axis=0, keepdims=True)
        vals.append(m)
        idxs.append(sel)
        x = jnp.where(idx == sel, -jnp.inf, x)
    return vals, idxs


def _peer_select_kernel(q_ref, keys_ref, w_ref, sel_t, sel_rows):
    step = pl.program_id(0)
    tm = q_ref.shape[1]
    k = PEER_TOPK
    slot = step % 2
    prev = 1 - slot
    per_iter = tm * PEER_SEL_HEADS // PEER_HEADS

    @pl.when(step == 0)
    def _():
        sel_rows[1] = jnp.zeros(sel_rows.shape[1:], F32)

    key_idx = lax.broadcasted_iota(jnp.int32, (N_KEYS, tm), 0).astype(F32)
    ncand = len(_PAIR_CANDIDATES)
    npad = -(-ncand // 8) * 8
    cand_row = lax.broadcasted_iota(jnp.int32, (npad, tm), 0)
    cand_idx = jnp.full((npad, tm), _BIG_INDEX, F32)
    for r, (a, b) in enumerate(_PAIR_CANDIDATES):
        cand_idx = jnp.where(cand_row == r, float(a * k + b), cand_idx)
    pad_rows = [jnp.full((1, tm), -jnp.inf, F32)] * (npad - ncand)
    sub = lax.broadcasted_iota(jnp.int32, (N_KEYS, _NSEL), 0).astype(F32)

    def select_head(h):
        tops = []
        for p in range(2):
            sc = lax.dot_general(keys_ref[h, p], q_ref[2 * h + p], TRANS_B, preferred_element_type=F32)
            tops.append(_topk_rows(sc, key_idx, k))
        (v0, i0), (v1, i1) = tops
        cand = jnp.concatenate([v0[a] + v1[b] for a, b in _PAIR_CANDIDATES] + pad_rows, axis=0)
        sel_s, sel_pos = _topk_rows(cand, cand_idx, k)
        sel_s = jnp.concatenate(sel_s, axis=0)
        sel_pos = jnp.concatenate(sel_pos, axis=0)
        a_sel = jnp.floor(sel_pos * (1.0 / k))
        b_sel = sel_pos - a_sel * k
        i_sel = jnp.zeros((k, tm), F32)
        j_sel = jnp.zeros((k, tm), F32)
        for r in range(k):
            i_sel = i_sel + jnp.where(a_sel == r, i0[r], 0.0)
            j_sel = j_sel + jnp.where(b_sel == r, i1[r], 0.0)
        e = jnp.exp(sel_s - sel_s[0:1, :])
        rows = pl.ds(pl.multiple_of(h * k, k), k)
        sel_t[0, rows, :] = i_sel
        sel_t[1, rows, :] = j_sel
        sel_t[2, rows, :] = e / jnp.sum(e, axis=0, keepdims=True)

    def build_mask(tk):
        row = pl.ds(tk, 1)
        eq_i = sub == sel_rows[prev, 0, row, :]
        eq_j = sub == sel_rows[prev, 1, row, :]
        lhs = jnp.concatenate([jnp.where(eq_i, sel_rows[prev, 2, row, :], 0.0),
                               jnp.where(eq_i, sel_rows[prev, 3, row, :], 0.0)], axis=1).astype(BF16)
        one = jnp.where(eq_j, 1.0, 0.0).astype(BF16)
        rhs = jnp.concatenate([one, one], axis=1)
        wt = lax.dot_general(lhs, rhs, TRANS_B, preferred_element_type=F32)
        w_ref[:, tk] = wt.reshape(N_KEYS // PEER_W_ROWS, PEER_W_ROWS, N_KEYS)

    def heads(it, carry):
        for hh in range(PEER_SEL_HEADS):
            select_head(it * PEER_SEL_HEADS + hh)
        for u in range(per_iter):
            build_mask(it * per_iter + u)
        return carry

    lax.fori_loop(0, PEER_HEADS // PEER_SEL_HEADS, heads, 0)

    gate = sel_t[2].T
    gate_hi = gate.astype(BF16).astype(F32)
    sel_rows[slot, 0] = sel_t[0].T
    sel_rows[slot, 1] = sel_t[1].T
    sel_rows[slot, 2] = gate_hi
    sel_rows[slot, 3] = gate - gate_hi


def peer_select(q, sub_keys, layer):
    nq, t, _ = q.shape
    tm = _row_tile(t, PEER_SEL_TM)
    nt = t // tm
    nib = N_KEYS // PEER_W_ROWS
    return pl.pallas_call(
        _peer_select_kernel, grid=(nt + 1,),
        in_specs=[pl.BlockSpec((nq, tm, HEAD_DIM), lambda s: (0, jnp.minimum(s, nt - 1), 0)),
                  pl.BlockSpec((None,) + sub_keys.shape[1:], lambda s: (layer, 0, 0, 0, 0))],
        out_specs=pl.BlockSpec((nib, tm, PEER_W_ROWS, N_KEYS), lambda s: (0, jnp.maximum(s - 1, 0), 0, 0)),
        out_shape=jax.ShapeDtypeStruct((nib, t, PEER_W_ROWS, N_KEYS), F32),
        scratch_shapes=[pltpu.VMEM((3, _NSEL, tm), F32), pltpu.VMEM((2, 4, tm, _NSEL), F32)],
        compiler_params=_params(("arbitrary",)), name="peer_select")(q, sub_keys)


PEER_TE = 512
PEER_TM = 1024


def _gelu(x):
    return 0.5 * x * (1.0 + lax.erf(x * math.sqrt(0.5)))


def _peer_experts_kernel(xn_ref, u_ref, v_ref, w_ref, o_ref, mix_scr, *, ne):
    s = pl.program_id(0)
    tm = xn_ref.shape[0]
    sub_blocks = PEER_TE // N_KEYS
    slot = s % 2
    e = s % ne

    @pl.when(s == 0)
    def _():
        o_ref[...] = jnp.zeros_like(o_ref)
        mix_scr[1] = jnp.zeros(mix_scr.shape[1:], mix_scr.dtype)

    starts_tile = jnp.logical_or(s == 0, (s - 1) % ne == 0)
    prev = jnp.where(starts_tile, 0.0, o_ref[...])
    o_ref[...] = prev + jnp.dot(mix_scr[1 - slot], v_ref[...], preferred_element_type=F32)

    h = lax.dot_general(xn_ref[...], u_ref[...], TRANS_B, preferred_element_type=F32)
    base = (e % (PEER_W_ROWS // sub_blocks)) * sub_blocks
    parts = []
    for ii in range(sub_blocks):
        wsub = w_ref[pl.ds(base + ii, tm, stride=PEER_W_ROWS), :]
        parts.append((wsub * _gelu(h[:, ii * N_KEYS:(ii + 1) * N_KEYS])).astype(BF16))
    mix_scr[slot] = jnp.concatenate(parts, axis=1)


def peer_experts(xn, u, v, w, layer):
    t, d = xn.shape
    ne = u.shape[1] // PEER_TE
    tm = _row_tile(t, PEER_TM)
    per_w = PEER_W_ROWS * N_KEYS // PEER_TE
    nt = t // tm
    steps = nt * ne
    w2 = w.reshape(-1, N_KEYS)
    cur = lambda s: jnp.minimum(s, steps - 1)
    last = lambda s: jnp.maximum(s - 1, 0)
    once = pl.Buffered(1)
    return pl.pallas_call(
        functools.partial(_peer_experts_kernel, ne=ne), grid=(steps + 1,),
        in_specs=[pl.BlockSpec((tm, d), lambda s: (cur(s) // ne, 0), pipeline_mode=once),
                  pl.BlockSpec((None, PEER_TE, d), lambda s: (layer, cur(s) % ne, 0)),
                  pl.BlockSpec((None, PEER_TE, d), lambda s: (layer, last(s) % ne, 0)),
                  pl.BlockSpec((tm * PEER_W_ROWS, N_KEYS),
                               lambda s: (((cur(s) % ne) // per_w) * nt + cur(s) // ne, 0))],
        out_specs=pl.BlockSpec((tm, d), lambda s: (last(s) // ne, 0), pipeline_mode=once),
        out_shape=jax.ShapeDtypeStruct((t, d), F32),
        scratch_shapes=[pltpu.VMEM((2, tm, PEER_TE), BF16)],
        compiler_params=_params(("arbitrary",)), name="peer_experts")(xn, u, v, w2)


def kernel(x_prompt, x_sample, norm_mix, w_in, ret_decay_fwd, ret_decay_bwd, w_branch_attn, w_branch_ret,
           w_gate, b_gate, w_out, norm_ffn, peer_w_q, peer_sub_keys, peer_u, peer_v, final_norm):
    assert x_prompt.shape[1:] == x_sample.shape[1:]
    bp, s, d = x_prompt.shape
    bs = x_sample.shape[0]
    batch = bp + bs
    t = batch * s
    depth = w_in.shape[0]
    assert w_in.shape[2] == IN_COLS and s % (ATT_DILATIONS[-1] * 2 * ATT_HALF_WINDOW) == 0

    w_in, w_gate, w_out, w_a, w_r, w_q, u, v = (
        cast_bf16(w) for w in (w_in, w_gate, w_out, w_branch_attn, w_branch_ret, peer_w_q, peer_u, peer_v))
    sub_keys = peer_sub_keys.astype(BF16)
    b_gate = b_gate.astype(F32).reshape(depth, 1, -1)
    tables = _rotation_tables(s)
    x = delta = None
    for l in range(depth):
        if l == 0:
            x, xn = rmsnorm_concat(x_prompt.reshape(bp * s, d), x_sample.reshape(bs * s, d), norm_mix[l])
        else:
            x, xn = add_rmsnorm(x, delta, norm_mix[l])
        att, ret = input_projection(xn, w_in, l, tables, s)
        o_a = dilated_attention(att, batch, s)
        decays = jnp.stack([ret_decay_fwd[l], ret_decay_bwd[l]]).astype(F32)
        o_b = retention(ret, decays, batch, s)
        merged = branch_merge(xn, o_a, o_b, w_gate, b_gate, w_a, w_r, l)
        x = residual_matmul(x, merged, w_out, l)
        xn2 = rmsnorm(x, norm_ffn[l], BF16)
        q = peer_query(xn2, w_q, l)
        w = peer_select(q, sub_keys, l)
        delta = peer_experts(xn2, u, v, w, l)
    y_p, y_s = add_rmsnorm_split(x, delta, final_norm, bp * s)
    return y_p.reshape(bp, s, d), y_s.reshape(bs, s, d)
```

```python
import functools
import math

import jax
import jax.numpy as jnp
from jax import lax
from jax.experimental import pallas as pl
from jax.experimental.pallas import tpu as pltpu

HEAD_DIM = 128
ATT_HEADS = 8
ATT_DILATIONS = (1, 4, 16)
ATT_HALF_WINDOW = 64
ATT_GROUP_WIDTH = ATT_HEADS * HEAD_DIM
ATT_PART = len(ATT_DILATIONS) * ATT_GROUP_WIDTH
ATT_COLS = 3 * ATT_PART
RET_HEADS = 8
RET_V_DIM = 256
RET_CHUNK = 128
RET_QK_WIDTH = RET_HEADS * HEAD_DIM
RET_V_WIDTH = RET_HEADS * RET_V_DIM
RET_COLS = 2 * RET_QK_WIDTH + 2 * RET_V_WIDTH
IN_COLS = ATT_COLS + RET_COLS
PEER_HEADS = 8
N_KEYS = 128
PEER_TOPK = 16
ROPE_THETA = 10000.0
RET_THETA = 10000.0
EPS = 1e-6
NEG_INF = -1e30

VMEM_LIMIT = 56 * 1024 * 1024

F32 = jnp.float32
BF16 = jnp.bfloat16
TRANS_B = (((1,), (1,)), ((), ()))


def _params(sem):
    return pltpu.CompilerParams(dimension_semantics=sem, vmem_limit_bytes=VMEM_LIMIT)


def _row_tile(t, cap):
    r = min(cap, t)
    assert t % r == 0
    return r


def _rms(x, g):
    return x * lax.rsqrt(jnp.mean(x * x, axis=-1, keepdims=True) + EPS) * g


def _rmsnorm_kernel(x_ref, g_ref, o_ref):
    o_ref[...] = _rms(x_ref[...], g_ref[...]).astype(o_ref.dtype)


def _add_rmsnorm_kernel(x_ref, d_ref, g_ref, s_ref, o_ref):
    x = x_ref[...] + d_ref[...]
    s_ref[...] = x
    o_ref[...] = _rms(x, g_ref[...]).astype(o_ref.dtype)


def _add_rmsnorm_split_kernel(x_ref, d_ref, g_ref, a_ref, b_ref, *, n_first):
    y = _rms(x_ref[...] + d_ref[...], g_ref[...])
    i = pl.program_id(0)

    @pl.when(i < n_first)
    def _():
        a_ref[...] = y

    @pl.when(i >= n_first)
    def _():
        b_ref[...] = y


def _rmsnorm_concat_kernel(a_ref, b_ref, g_ref, x_ref, o_ref, *, n_first):
    i = pl.program_id(0)

    def emit(src_ref):
        x = src_ref[...]
        x_ref[...] = x
        o_ref[...] = _rms(x, g_ref[...]).astype(o_ref.dtype)

    pl.when(i < n_first)(lambda: emit(a_ref))
    pl.when(i >= n_first)(lambda: emit(b_ref))


def rmsnorm_concat(xa, xb, gain):
    ta, d = xa.shape
    tb = xb.shape[0]
    tr = _row_tile(math.gcd(ta, tb), 256)
    nf = ta // tr
    row = pl.BlockSpec((tr, d), lambda i: (i, 0))
    return pl.pallas_call(
        functools.partial(_rmsnorm_concat_kernel, n_first=nf), grid=((ta + tb) // tr,),
        in_specs=[pl.BlockSpec((tr, d), lambda i: (jnp.minimum(i, nf - 1), 0)),
                  pl.BlockSpec((tr, d), lambda i: (jnp.maximum(i - nf, 0), 0)),
                  pl.BlockSpec((1, d), lambda i: (0, 0))],
        out_specs=[row, row],
        out_shape=[jax.ShapeDtypeStruct((ta + tb, d), F32), jax.ShapeDtypeStruct((ta + tb, d), BF16)],
        compiler_params=_params(("arbitrary",)), name="rmsnorm_concat")(xa, xb, gain.reshape(1, d))


def _cast_kernel(x_ref, o_ref):
    o_ref[...] = x_ref[...].astype(o_ref.dtype)


def cast_bf16(w):
    cols = w.shape[-1]
    w2 = w.reshape(-1, cols)
    rows = w2.shape[0]
    tr = _row_tile(rows, 512)
    tc = max(c for c in range(128, min(cols, 4096) + 1, 128) if cols % c == 0)
    blk = pl.BlockSpec((tr, tc), lambda i, j: (i, j))
    out = pl.pallas_call(
        _cast_kernel, grid=(rows // tr, cols // tc), in_specs=[blk], out_specs=blk,
        out_shape=jax.ShapeDtypeStruct((rows, cols), BF16),
        compiler_params=_params(("parallel", "parallel")), name="cast_bf16")(w2)
    return out.reshape(w.shape)


def rmsnorm(x, gain, out_dtype):
    t, d = x.shape
    tr = _row_tile(t, 256)
    row = pl.BlockSpec((tr, d), lambda i: (i, 0))
    return pl.pallas_call(
        _rmsnorm_kernel, grid=(t // tr,),
        in_specs=[row, pl.BlockSpec((1, d), lambda i: (0, 0))],
        out_specs=row, out_shape=jax.ShapeDtypeStruct((t, d), out_dtype),
        compiler_params=_params(("parallel",)), name="rmsnorm")(x, gain.reshape(1, d))


def add_rmsnorm(x, delta, gain):
    t, d = x.shape
    tr = _row_tile(t, 256)
    row = pl.BlockSpec((tr, d), lambda i: (i, 0))
    return pl.pallas_call(
        _add_rmsnorm_kernel, grid=(t // tr,),
        in_specs=[row, row, pl.BlockSpec((1, d), lambda i: (0, 0))],
        out_specs=[row, row],
        out_shape=[jax.ShapeDtypeStruct((t, d), F32), jax.ShapeDtypeStruct((t, d), BF16)],
        compiler_params=_params(("parallel",)), name="add_rmsnorm")(x, delta, gain.reshape(1, d))


def add_rmsnorm_split(x, delta, gain, t_first):
    t, d = x.shape
    tr = _row_tile(math.gcd(t_first, t - t_first), 256)
    nf = t_first // tr
    row = pl.BlockSpec((tr, d), lambda i: (i, 0))
    return pl.pallas_call(
        functools.partial(_add_rmsnorm_split_kernel, n_first=nf), grid=(t // tr,),
        in_specs=[row, row, pl.BlockSpec((1, d), lambda i: (0, 0))],
        out_specs=[pl.BlockSpec((tr, d), lambda i: (jnp.minimum(i, nf - 1), 0)),
                   pl.BlockSpec((tr, d), lambda i: (jnp.maximum(i - nf, 0), 0))],
        out_shape=[jax.ShapeDtypeStruct((t_first, d), F32), jax.ShapeDtypeStruct((t - t_first, d), F32)],
        compiler_params=_params(("arbitrary",)), name="add_rmsnorm_split")(x, delta, gain.reshape(1, d))


PROJ_TN = 1024
PROJ_CHUNK = 256
_ROPE_TILES = 2 * ATT_PART // PROJ_TN
_RETROT_TILES = 2 * RET_QK_WIDTH // PROJ_TN
_RETK_TILE = RET_QK_WIDTH // PROJ_TN


def _proj_att_kernel(x_ref, w_ref, cos_ref, sin_ref, o_ref):
    rotate = pl.program_id(1) < _ROPE_TILES
    cos = cos_ref[...]
    sin = sin_ref[...]
    x = x_ref[...]
    for c0 in range(0, PROJ_TN, PROJ_CHUNK):
        acc = jnp.dot(x, w_ref[:, c0:c0 + PROJ_CHUNK], preferred_element_type=F32)
        for h0 in range(0, PROJ_CHUNK, HEAD_DIM):
            xs = acc[:, h0:h0 + HEAD_DIM]
            rot = xs * cos + pltpu.roll(xs, HEAD_DIM // 2, axis=1) * sin
            o_ref[:, c0 + h0:c0 + h0 + HEAD_DIM] = jnp.where(rotate, rot, xs).astype(o_ref.dtype)


def _proj_ret_kernel(x_ref, w_ref, cos_ref, sin_ref, o_ref):
    j = pl.program_id(1)
    rotate = j < _RETROT_TILES
    scale = jnp.where(j == _RETK_TILE, HEAD_DIM ** -0.5, 1.0).astype(F32)
    cos = cos_ref[...]
    sin = sin_ref[...]
    x = x_ref[...]
    even = (lax.broadcasted_iota(jnp.int32, (x.shape[0], HEAD_DIM), 1) % 2) == 0
    for c0 in range(0, PROJ_TN, PROJ_CHUNK):
        acc = jnp.dot(x, w_ref[:, c0:c0 + PROJ_CHUNK], preferred_element_type=F32)
        for h0 in range(0, PROJ_CHUNK, HEAD_DIM):
            xs = acc[:, h0:h0 + HEAD_DIM]
            sw = jnp.where(even, pltpu.roll(xs, HEAD_DIM - 1, axis=1), pltpu.roll(xs, 1, axis=1))
            rot = (xs * cos + sw * sin) * scale
            o_ref[:, c0 + h0:c0 + h0 + HEAD_DIM] = jnp.where(rotate, rot, xs).astype(o_ref.dtype)


def _rotation_tables(s):
    pos = jnp.arange(s, dtype=F32)[:, None]
    inv = ROPE_THETA ** (-jnp.arange(0, HEAD_DIM, 2, dtype=F32) / HEAD_DIM)
    ang = pos * inv[None, :]
    rc = jnp.concatenate([jnp.cos(ang), jnp.cos(ang)], axis=-1)
    rs = jnp.concatenate([-jnp.sin(ang), jnp.sin(ang)], axis=-1)
    inv_r = 1.0 / (RET_THETA ** jnp.linspace(0.0, 1.0, HEAD_DIM // 2, dtype=F32))
    ang_r = pos * inv_r[None, :]
    qc = jnp.repeat(jnp.cos(ang_r), 2, axis=-1)
    qs = jnp.stack([-jnp.sin(ang_r), jnp.sin(ang_r)], axis=-1).reshape(s, HEAD_DIM)
    return rc, rs, qc, qs


def input_projection(xn, w_in, layer, tables, s):
    t, d = xn.shape
    tm = _row_tile(s, 1024)
    nsb = s // tm
    tab = pl.BlockSpec((tm, HEAD_DIM), lambda i, j: (i % nsb, 0))
    xspec = pl.BlockSpec((tm, d), lambda i, j: (i, 0))
    ospec = pl.BlockSpec((tm, PROJ_TN), lambda i, j: (i, j))
    att = pl.pallas_call(
        _proj_att_kernel, grid=(t // tm, ATT_COLS // PROJ_TN),
        in_specs=[xspec, pl.BlockSpec((None, d, PROJ_TN), lambda i, j: (layer, 0, j)), tab, tab],
        out_specs=ospec, out_shape=jax.ShapeDtypeStruct((t, ATT_COLS), F32),
        compiler_params=_params(("parallel", "arbitrary")), name="proj_attention")(xn, w_in, *tables[:2])
    ret0 = ATT_COLS // PROJ_TN
    ret = pl.pallas_call(
        _proj_ret_kernel, grid=(t // tm, RET_COLS // PROJ_TN),
        in_specs=[xspec, pl.BlockSpec((None, d, PROJ_TN), lambda i, j: (layer, 0, ret0 + j)), tab, tab],
        out_specs=ospec, out_shape=jax.ShapeDtypeStruct((t, RET_COLS), BF16),
        compiler_params=_params(("parallel", "arbitrary")), name="proj_retention")(xn, w_in, *tables[2:])
    return att, ret


ATT_BLOCKS = 16
ATT_UNROLL = 8
ATT_PERM_ROWS = 256


def _rows(start, size, stride):
    return pl.ds(start, size) if stride == 1 else pl.ds(start, size, stride=stride)


def _att_kernel(q0, q1, q2, k0, k1, k2, v0, v1, v2, o_ref, og0, og1, og2, lg0, lg1, lg2, cls_ref, *, seq):
    w = ATT_HALF_WINDOW
    scale = HEAD_DIM ** -0.5
    groups = ((q0, k0, v0, og0, lg0), (q1, k1, v1, og1, lg1), (q2, k2, v2, og2, lg2))
    for (q_ref, k_ref, v_ref, og_ref, lg_ref), dil in zip(groups, ATT_DILATIONS):
        sub_len = seq // dil
        bq = min(128, sub_len)
        kw = min(bq + 2 * w, sub_len)
        nqb = sub_len // bq
        assert nqb * dil == ATT_BLOCKS
        rel = (lax.broadcasted_iota(jnp.int32, (bq, kw), 1)
               - lax.broadcasted_iota(jnp.int32, (bq, kw), 0))
        by_class = nqb == 1 and dil * dil <= ATT_PERM_ROWS and seq % ATT_PERM_ROWS == 0
        if by_class:
            per = ATT_PERM_ROWS // dil
            r = lax.broadcasted_iota(jnp.int32, (ATT_PERM_ROWS, ATT_PERM_ROWS), 0)
            c = lax.broadcasted_iota(jnp.int32, (ATT_PERM_ROWS, ATT_PERM_ROWS), 1)
            perm = jnp.where(c == (r % per) * dil + r // per, 1.0, 0.0).astype(BF16)
            for t0 in range(0, seq, ATT_PERM_ROWS):
                rows = slice(t0, t0 + ATT_PERM_ROWS)
                tile = jnp.concatenate([ref[rows, :].astype(BF16) for ref in (q_ref, k_ref, v_ref)], axis=1)
                moved = jnp.dot(perm, tile, preferred_element_type=F32).astype(BF16)
                for part in range(3):
                    cls_ref[part, rows, :] = moved[:, part * HEAD_DIM:(part + 1) * HEAD_DIM]

        def scores(it, q_ref=q_ref, k_ref=k_ref, v_ref=v_ref, dil=dil, sub_len=sub_len, bq=bq, kw=kw,
                   nqb=nqb, rel=rel, by_class=by_class):
            res = it // nqb
            qs = (it % nqb) * bq
            ks = jnp.clip(qs - w, 0, sub_len - kw)
            if by_class:
                per = ATT_PERM_ROWS // dil

                def gather(part):
                    return jnp.concatenate(
                        [cls_ref[part, pl.ds(pl.multiple_of(t0 + res * per, per), per), :]
                         for t0 in range(0, seq, ATT_PERM_ROWS)], axis=0)

                q, k, v = gather(0), gather(1), gather(2)
            else:
                q = q_ref[_rows(qs * dil + res, bq, dil), :].astype(BF16)
                k = k_ref[_rows(ks * dil + res, kw, dil), :].astype(BF16)
                v = v_ref[_rows(ks * dil + res, kw, dil), :].astype(BF16)
            s = lax.dot_general(q, k, TRANS_B, preferred_element_type=F32) * scale
            s = jnp.where(jnp.abs(rel + (ks - qs)) <= w, s, NEG_INF)
            return s, v, _rows(qs * dil + res, bq, dil)

        def blocks(it, carry, scores=scores, og_ref=og_ref, lg_ref=lg_ref, bq=bq):
            staged = [scores(it * ATT_UNROLL + u) for u in range(ATT_UNROLL)]
            probs = []
            for s, _, _ in staged:
                m = jnp.max(s, axis=-1, keepdims=True)
                e = jnp.exp(s - m)
                l = jnp.sum(e, axis=-1, keepdims=True)
                probs.append(((e * (1.0 / l)).astype(BF16), m + jnp.log(l)))
            for (p, lse), (_, v, rows) in zip(probs, staged):
                og_ref[rows, :] = jnp.dot(p, v, preferred_element_type=F32)
                lg_ref[rows, :] = jnp.broadcast_to(lse, (bq, HEAD_DIM))
            return carry

        lax.fori_loop(0, ATT_BLOCKS // ATT_UNROLL, blocks, 0)

    mb = min(256, seq)

    def merge(c, carry):
        rows = pl.ds(pl.multiple_of(c * mb, mb), mb)
        a, b, cc = lg0[rows, :], lg1[rows, :], lg2[rows, :]
        m = jnp.maximum(jnp.maximum(a, b), cc)
        ea, eb, ec = jnp.exp(a - m), jnp.exp(b - m), jnp.exp(cc - m)
        z = ea + eb + ec
        o_ref[rows, :] = ((ea / z) * og0[rows, :] + (eb / z) * og1[rows, :]
                          + (ec / z) * og2[rows, :]).astype(o_ref.dtype)
        return carry

    lax.fori_loop(0, seq // mb, merge, 0)


def dilated_attention(att, batch, s):
    t = batch * s

    def spec(part, group):
        base = (part * ATT_PART + group * ATT_GROUP_WIDTH) // HEAD_DIM
        return pl.BlockSpec((s, HEAD_DIM), lambda b, h: (b, base + h))

    ng = len(ATT_DILATIONS)
    return pl.pallas_call(
        functools.partial(_att_kernel, seq=s), grid=(batch, ATT_HEADS),
        in_specs=[spec(p, g) for p in range(3) for g in range(ng)],
        out_specs=pl.BlockSpec((s, HEAD_DIM), lambda b, h: (b, h)),
        out_shape=jax.ShapeDtypeStruct((t, ATT_GROUP_WIDTH), BF16),
        scratch_shapes=[pltpu.VMEM((s, HEAD_DIM), F32)] * (2 * ng) + [pltpu.VMEM((3, s, HEAD_DIM), BF16)],
        compiler_params=_params(("parallel", "parallel")), name="dilated_attention")(*([att] * (3 * ng)))


def _log_sigmoid(x):
    return jnp.minimum(x, 0.0) - jnp.log1p(jnp.exp(-jnp.abs(x)))


RET_STEP_HEADS = 4


def _retention_kernel(dec_ref, q_ref, k_ref, v_ref, g_ref, o_ref, acc_ref, accb_ref, *, seq):
    c = RET_CHUNK
    nc = seq // c
    hs = RET_STEP_HEADS
    h0 = pl.program_id(1) * hs
    row = lax.broadcasted_iota(jnp.int32, (c, RET_V_DIM), 0).astype(F32)
    row_k = lax.broadcasted_iota(jnp.int32, (c, HEAD_DIM), 0).astype(F32)
    rel = (lax.broadcasted_iota(jnp.int32, (c, c), 0) - lax.broadcasted_iota(jnp.int32, (c, c), 1)).astype(F32)
    consts = []
    for hh in range(hs):
        lgf = _log_sigmoid(jnp.full((1, 1), dec_ref[0, h0 + hh], F32))
        lgb = _log_sigmoid(jnp.full((1, 1), dec_ref[1, h0 + hh], F32))
        consts.append(dict(
            decay=jnp.where(rel >= 0, jnp.exp(jnp.maximum(rel, 0.0) * lgf), jnp.exp(jnp.maximum(-rel, 0.0) * lgb)),
            qdec_f=jnp.exp((row + 1.0) * lgf),
            qdec_b=jnp.exp((c - row) * lgb),
            kdec_f=jnp.exp((c - 1.0 - row_k) * lgf),
            kdec_b=jnp.exp(row_k * lgb),
            chunk_f=jnp.exp(c * lgf), chunk_b=jnp.exp(c * lgb)))

    def load(ci, hh):
        rows = pl.ds(pl.multiple_of(ci * c, c), c)
        qk = slice(hh * HEAD_DIM, (hh + 1) * HEAD_DIM)
        vv = slice(hh * RET_V_DIM, (hh + 1) * RET_V_DIM)
        return rows, vv, q_ref[rows, qk], k_ref[rows, qk], v_ref[rows, vv]

    def scans(step, states):
        out, pending = [], []
        for hh, (state_f, state_b) in enumerate(states):
            cn = consts[hh]
            rows, vv, q, k, v = load(step, hh)
            a = lax.dot_general(q, k, TRANS_B, preferred_element_type=F32)
            cross = jnp.dot(q, state_f.astype(BF16), preferred_element_type=F32) * cn["qdec_f"]
            kd = (k.astype(F32) * cn["kdec_f"]).T.astype(BF16)
            state_f = state_f * cn["chunk_f"] + jnp.dot(kd, v, preferred_element_type=F32)

            rows1, _, q1, k1, v1 = load(nc - 1 - step, hh)
            accb_ref[rows1, vv] = jnp.dot(q1, state_b.astype(BF16), preferred_element_type=F32) * cn["qdec_b"]
            kd1 = (k1.astype(F32) * cn["kdec_b"]).T.astype(BF16)
            state_b = state_b * cn["chunk_b"] + jnp.dot(kd1, v1, preferred_element_type=F32)
            out.append((state_f, state_b))
            pending.append((rows, vv, (a * cn["decay"]).astype(BF16), v, cross))
        for rows, vv, a_dec, v, cross in pending:
            acc_ref[rows, vv] = jnp.dot(a_dec, v, preferred_element_type=F32) + cross
        return tuple(out)

    zero = jnp.zeros((HEAD_DIM, RET_V_DIM), F32)
    lax.fori_loop(0, nc, scans, ((zero, zero),) * hs)

    def finish(ci, carry):
        rows = pl.ds(pl.multiple_of(ci * c, c), c)
        for hh in range(hs):
            vv = slice(hh * RET_V_DIM, (hh + 1) * RET_V_DIM)
            y = acc_ref[rows, vv] + accb_ref[rows, vv]
            y = y * lax.rsqrt(jnp.mean(y * y, axis=-1, keepdims=True) + EPS)
            g = g_ref[rows, vv].astype(F32)
            o_ref[rows, vv] = (g * jax.nn.sigmoid(g) * y).astype(o_ref.dtype)
        return carry

    lax.fori_loop(0, nc, finish, 0)


def retention(ret, decays, batch, s):
    t = batch * s
    hs = RET_STEP_HEADS
    qw, vw = hs * HEAD_DIM, hs * RET_V_DIM
    qk = lambda base: pl.BlockSpec((s, qw), lambda b, h: (b, base // qw + h))
    vg = lambda base: pl.BlockSpec((s, vw), lambda b, h: (b, base // vw + h))
    return pl.pallas_call(
        functools.partial(_retention_kernel, seq=s),
        grid=(batch, RET_HEADS // hs),
        in_specs=[pl.BlockSpec(memory_space=pltpu.SMEM),
                  qk(0), qk(RET_QK_WIDTH), vg(2 * RET_QK_WIDTH), vg(2 * RET_QK_WIDTH + RET_V_WIDTH)],
        out_specs=pl.BlockSpec((s, vw), lambda b, h: (b, h)),
        out_shape=jax.ShapeDtypeStruct((t, RET_V_WIDTH), BF16),
        scratch_shapes=[pltpu.VMEM((s, vw), F32)] * 2,
        compiler_params=_params(("parallel", "parallel")), name="retention")(decays, ret, ret, ret, ret)


def _branch_merge_kernel(xn_ref, oa_ref, ob_ref, wga_ref, wgr_ref, wa_ref, wr_ref, ba_ref, br_ref, o_ref):
    xn = xn_ref[...]
    ga = jax.nn.sigmoid(jnp.dot(xn, wga_ref[...], preferred_element_type=F32) + ba_ref[...])
    gr = jax.nn.sigmoid(jnp.dot(xn, wgr_ref[...], preferred_element_type=F32) + br_ref[...])
    a = jnp.dot(oa_ref[...], wa_ref[...], preferred_element_type=F32)
    r = jnp.dot(ob_ref[...], wr_ref[...], preferred_element_type=F32)
    o_ref[...] = (ga * a + gr * r).astype(o_ref.dtype)


def branch_merge(xn, o_a, o_b, w_gate, b_gate, w_a, w_r, layer):
    t, d = xn.shape
    tm = _row_tile(t, 512)
    tn = min(512, d)
    nb = d // tn
    res = lambda width: pl.BlockSpec((tm, width), lambda i, n: (i, 0))
    return pl.pallas_call(
        _branch_merge_kernel, grid=(t // tm, nb),
        in_specs=[res(d), res(o_a.shape[1]), res(o_b.shape[1]),
                  pl.BlockSpec((None, d, tn), lambda i, n: (layer, 0, n)),
                  pl.BlockSpec((None, d, tn), lambda i, n: (layer, 0, nb + n)),
                  pl.BlockSpec((None, o_a.shape[1], tn), lambda i, n: (layer, 0, n)),
                  pl.BlockSpec((None, o_b.shape[1], tn), lambda i, n: (layer, 0, n)),
                  pl.BlockSpec((None, 1, tn), lambda i, n: (layer, 0, n)),
                  pl.BlockSpec((None, 1, tn), lambda i, n: (layer, 0, nb + n))],
        out_specs=pl.BlockSpec((tm, tn), lambda i, n: (i, n)),
        out_shape=jax.ShapeDtypeStruct((t, d), BF16),
        compiler_params=_params(("parallel", "arbitrary")), name="branch_merge")(
            xn, o_a, o_b, w_gate, w_gate, w_a, w_r, b_gate, b_gate)


def _residual_matmul_kernel(x_ref, m_ref, w_ref, o_ref):
    o_ref[...] = x_ref[...] + jnp.dot(m_ref[...], w_ref[...], preferred_element_type=F32)


def residual_matmul(x, merged, w_out, layer):
    t, d = x.shape
    tm = _row_tile(t, 1024)
    tn = min(1024, d)
    return pl.pallas_call(
        _residual_matmul_kernel, grid=(t // tm, d // tn),
        in_specs=[pl.BlockSpec((tm, tn), lambda i, n: (i, n)),
                  pl.BlockSpec((tm, d), lambda i, n: (i, 0)),
                  pl.BlockSpec((None, d, tn), lambda i, n: (layer, 0, n))],
        out_specs=pl.BlockSpec((tm, tn), lambda i, n: (i, n)),
        out_shape=jax.ShapeDtypeStruct((t, d), F32),
        compiler_params=_params(("parallel", "arbitrary")), name="residual_matmul")(x, merged, w_out)


PEER_SEL_TM = 128
PEER_SEL_HEADS = 4
PEER_W_ROWS = 8
_BIG_INDEX = float(1 << 20)
_NSEL = PEER_HEADS * PEER_TOPK

_PAIR_CANDIDATES = [(a, b) for a in range(PEER_TOPK) for b in range(PEER_TOPK)
                    if (a + 1) * (b + 1) <= PEER_TOPK]


def _peer_query_kernel(x_ref, w_ref, o_ref):
    acc = jnp.dot(x_ref[...], w_ref[...], preferred_element_type=F32).astype(o_ref.dtype)
    for c in range(o_ref.shape[0]):
        o_ref[c] = acc[:, c * HEAD_DIM:(c + 1) * HEAD_DIM]


def peer_query(x, w, layer):
    t, d = x.shape
    n = w.shape[2]
    tm = _row_tile(t, 1024)
    tn = min(1024, n)
    return pl.pallas_call(
        _peer_query_kernel, grid=(t // tm, n // tn),
        in_specs=[pl.BlockSpec((tm, d), lambda i, j: (i, 0)),
                  pl.BlockSpec((None, d, tn), lambda i, j: (layer, 0, j))],
        out_specs=pl.BlockSpec((tn // HEAD_DIM, tm, HEAD_DIM), lambda i, j: (j, i, 0)),
        out_shape=jax.ShapeDtypeStruct((n // HEAD_DIM, t, HEAD_DIM), BF16),
        compiler_params=_params(("parallel", "arbitrary")), name="peer_query")(x, w)


def _topk_rows(x, idx, k):
    vals, idxs = [], []
    for _ in range(k):
        m = jnp.max(x, axis=0, keepdims=True)
        sel = jnp.min(jnp.where(x == m, idx, _BIG_INDEX), axis=0, keepdims=True)
        vals.append(m)
        idxs.append(sel)
        x = jnp.where(idx == sel, -jnp.inf, x)
    return vals, idxs


def _peer_select_kernel(q_ref, keys_ref, w_ref, sel_t, sel_rows):
    step = pl.program_id(0)
    tm = q_ref.shape[1]
    k = PEER_TOPK
    slot = step % 2
    prev = 1 - slot
    per_iter = tm * PEER_SEL_HEADS // PEER_HEADS

    @pl.when(step == 0)
    def _():
        sel_rows[1] = jnp.zeros(sel_rows.shape[1:], F32)

    key_idx = lax.broadcasted_iota(jnp.int32, (N_KEYS, tm), 0).astype(F32)
    ncand = len(_PAIR_CANDIDATES)
    npad = -(-ncand // 8) * 8
    cand_row = lax.broadcasted_iota(jnp.int32, (npad, tm), 0)
    cand_idx = jnp.full((npad, tm), _BIG_INDEX, F32)
    for r, (a, b) in enumerate(_PAIR_CANDIDATES):
        cand_idx = jnp.where(cand_row == r, float(a * k + b), cand_idx)
    pad_rows = [jnp.full((1, tm), -jnp.inf, F32)] * (npad - ncand)
    sub = lax.broadcasted_iota(jnp.int32, (N_KEYS, _NSEL), 0).astype(F32)

    def select_head(h):
        tops = []
        for p in range(2):
            sc = lax.dot_general(keys_ref[h, p], q_ref[2 * h + p], TRANS_B, preferred_element_type=F32)
            tops.append(_topk_rows(sc, key_idx, k))
        (v0, i0), (v1, i1) = tops
        cand = jnp.concatenate([v0[a] + v1[b] for a, b in _PAIR_CANDIDATES] + pad_rows, axis=0)
        sel_s, sel_pos = _topk_rows(cand, cand_idx, k)
        sel_s = jnp.concatenate(sel_s, axis=0)
        sel_pos = jnp.concatenate(sel_pos, axis=0)
        a_sel = jnp.floor(sel_pos * (1.0 / k))
        b_sel = sel_pos - a_sel * k
        i_sel = jnp.zeros((k, tm), F32)
        j_sel = jnp.zeros((k, tm), F32)
        for r in range(k):
            i_sel = i_sel + jnp.where(a_sel == r, i0[r], 0.0)
            j_sel = j_sel + jnp.where(b_sel == r, i1[r], 0.0)
        e = jnp.exp(sel_s - sel_s[0:1, :])
        rows = pl.ds(pl.multiple_of(h * k, k), k)
        sel_t[0, rows, :] = i_sel
        sel_t[1, rows, :] = j_sel
        sel_t[2, rows, :] = e / jnp.sum(e, axis=0, keepdims=True)

    def build_mask(tk):
        row = pl.ds(tk, 1)
        eq_i = sub == sel_rows[prev, 0, row, :]
        eq_j = sub == sel_rows[prev, 1, row, :]
        lhs = jnp.concatenate([jnp.where(eq_i, sel_rows[prev, 2, row, :], 0.0),
                               jnp.where(eq_i, sel_rows[prev, 3, row, :], 0.0)], axis=1).astype(BF16)
        one = jnp.where(eq_j, 1.0, 0.0).astype(BF16)
        rhs = jnp.concatenate([one, one], axis=1)
        wt = lax.dot_general(lhs, rhs, TRANS_B, preferred_element_type=F32)
        w_ref[:, tk] = wt.reshape(N_KEYS // PEER_W_ROWS, PEER_W_ROWS, N_KEYS)

    def heads(it, carry):
        for hh in range(PEER_SEL_HEADS):
            select_head(it * PEER_SEL_HEADS + hh)
        for u in range(per_iter):
            build_mask(it * per_iter + u)
        return carry

    lax.fori_loop(0, PEER_HEADS // PEER_SEL_HEADS, heads, 0)

    gate = sel_t[2].T
    gate_hi = gate.astype(BF16).astype(F32)
    sel_rows[slot, 0] = sel_t[0].T
    sel_rows[slot, 1] = sel_t[1].T
    sel_rows[slot, 2] = gate_hi
    sel_rows[slot, 3] = gate - gate_hi


def peer_select(q, sub_keys, layer):
    nq, t, _ = q.shape
    tm = _row_tile(t, PEER_SEL_TM)
    nt = t // tm
    nib = N_KEYS // PEER_W_ROWS
    return pl.pallas_call(
        _peer_select_kernel, grid=(nt + 1,),
        in_specs=[pl.BlockSpec((nq, tm, HEAD_DIM), lambda s: (0, jnp.minimum(s, nt - 1), 0)),
                  pl.BlockSpec((None,) + sub_keys.shape[1:], lambda s: (layer, 0, 0, 0, 0))],
        out_specs=pl.BlockSpec((nib, tm, PEER_W_ROWS, N_KEYS), lambda s: (0, jnp.maximum(s - 1, 0), 0, 0)),
        out_shape=jax.ShapeDtypeStruct((nib, t, PEER_W_ROWS, N_KEYS), F32),
        scratch_shapes=[pltpu.VMEM((3, _NSEL, tm), F32), pltpu.VMEM((2, 4, tm, _NSEL), F32)],
        compiler_params=_params(("arbitrary",)), name="peer_select")(q, sub_keys)


PEER_TE = 512
PEER_TM = 1024


def _gelu(x):
    return 0.5 * x * (1.0 + lax.erf(x * math.sqrt(0.5)))


def _peer_experts_kernel(xn_ref, u_ref, v_ref, w_ref, o_ref, mix_scr, *, ne):
    s = pl.program_id(0)
    tm = xn_ref.shape[0]
    sub_blocks = PEER_TE // N_KEYS
    slot = s % 2
    e = s % ne

    @pl.when(s == 0)
    def _():
        o_ref[...] = jnp.zeros_like(o_ref)
        mix_scr[1] = jnp.zeros(mix_scr.shape[1:], mix_scr.dtype)

    starts_tile = jnp.logical_or(s == 0, (s - 1) % ne == 0)
    prev = jnp.where(starts_tile, 0.0, o_ref[...])
    o_ref[...] = prev + jnp.dot(mix_scr[1 - slot], v_ref[...], preferred_element_type=F32)

    h = lax.dot_general(xn_ref[...], u_ref[...], TRANS_B, preferred_element_type=F32)
    base = (e % (PEER_W_ROWS // sub_blocks)) * sub_blocks
    parts = []
    for ii in range(sub_blocks):
        wsub = w_ref[pl.ds(base + ii, tm, stride=PEER_W_ROWS), :]
        parts.append((wsub * _gelu(h[:, ii * N_KEYS:(ii + 1) * N_KEYS])).astype(BF16))
    mix_scr[slot] = jnp.concatenate(parts, axis=1)


def peer_experts(xn, u, v, w, layer):
    t, d = xn.shape
    ne = u.shape[1] // PEER_TE
    tm = _row_tile(t, PEER_TM)
    per_w = PEER_W_ROWS * N_KEYS // PEER_TE
    nt = t // tm
    steps = nt * ne
    w2 = w.reshape(-1, N_KEYS)
    cur = lambda s: jnp.minimum(s, steps - 1)
    last = lambda s: jnp.maximum(s - 1, 0)
    once = pl.Buffered(1)
    return pl.pallas_call(
        functools.partial(_peer_experts_kernel, ne=ne), grid=(steps + 1,),
        in_specs=[pl.BlockSpec((tm, d), lambda s: (cur(s) // ne, 0), pipeline_mode=once),
                  pl.BlockSpec((None, PEER_TE, d), lambda s: (layer, cur(s) % ne, 0)),
                  pl.BlockSpec((None, PEER_TE, d), lambda s: (layer, last(s) % ne, 0)),
                  pl.BlockSpec((tm * PEER_W_ROWS, N_KEYS),
                               lambda s: (((cur(s) % ne) // per_w) * nt + cur(s) // ne, 0))],
        out_specs=pl.BlockSpec((tm, d), lambda s: (last(s) // ne, 0), pipeline_mode=once),
        out_shape=jax.ShapeDtypeStruct((t, d), F32),
        scratch_shapes=[pltpu.VMEM((2, tm, PEER_TE), BF16)],
        compiler_params=_params(("arbitrary",)), name="peer_experts")(xn, u, v, w2)


def kernel(x_prompt, x_sample, norm_mix, w_in, ret_decay_fwd, ret_decay_bwd, w_branch_attn, w_branch_ret,
           w_gate, b_gate, w_out, norm_ffn, peer_w_q, peer_sub_keys, peer_u, peer_v, final_norm):
    assert x_prompt.shape[1:] == x_sample.shape[1:]
    bp, s, d = x_prompt.shape
    bs = x_sample.shape[0]
    batch = bp + bs
    t = batch * s
    depth = w_in.shape[0]
    assert w_in.shape[2] == IN_COLS and s % (ATT_DILATIONS[-1] * 2 * ATT_HALF_WINDOW) == 0

    w_in, w_gate, w_out, w_a, w_r, w_q, u, v = (
        cast_bf16(w) for w in (w_in, w_gate, w_out, w_branch_attn, w_branch_ret, peer_w_q, peer_u, peer_v))
    sub_keys = peer_sub_keys.astype(BF16)
    b_gate = b_gate.astype(F32).reshape(depth, 1, -1)
    tables = _rotation_tables(s)
    x = delta = None
    for l in range(depth):
        if l == 0:
            x, xn = rmsnorm_concat(x_prompt.reshape(bp * s, d), x_sample.reshape(bs * s, d), norm_mix[l])
        else:
            x, xn = add_rmsnorm(x, delta, norm_mix[l])
        att, ret = input_projection(xn, w_in, l, tables, s)
        o_a = dilated_attention(att, batch, s)
        decays = jnp.stack([ret_decay_fwd[l], ret_decay_bwd[l]]).astype(F32)
        o_b = retention(ret, decays, batch, s)
        merged = branch_merge(xn, o_a, o_b, w_gate, b_gate, w_a, w_r, l)
        x = residual_matmul(x, merged, w_out, l)
        xn2 = rmsnorm(x, norm_ffn[l], BF16)
        q = peer_query(xn2, w_q, l)
        w = peer_select(q, sub_keys, l)
        delta = peer_experts(xn2, u, v, w, l)
    y_p, y_s = add_rmsnorm_split(x, delta, final_norm, bp * s)
    return y_p.reshape(bp, s, d), y_s.reshape(bs, s, d)
```

```python
import functools
import math

import jax
import jax.numpy as jnp
from jax import lax
from jax.experimental import pallas as pl
from jax.experimental.pallas import tpu as pltpu

HEAD_DIM = 128
ATT_HEADS = 8
ATT_DILATIONS = (1, 4, 16)
ATT_HALF_WINDOW = 64
ATT_GROUP_WIDTH = ATT_HEADS * HEAD_DIM
ATT_PART = len(ATT_DILATIONS) * ATT_GROUP_WIDTH
ATT_COLS = 3 * ATT_PART
RET_HEADS = 8
RET_V_DIM = 256
RET_CHUNK = 128
RET_QK_WIDTH = RET_HEADS * HEAD_DIM
RET_V_WIDTH = RET_HEADS * RET_V_DIM
RET_COLS = 2 * RET_QK_WIDTH + 2 * RET_V_WIDTH
IN_COLS = ATT_COLS + RET_COLS
PEER_HEADS = 8
N_KEYS = 128
PEER_TOPK = 16
ROPE_THETA = 10000.0
RET_THETA = 10000.0
EPS = 1e-6
NEG_INF = -1e30

VMEM_LIMIT = 56 * 1024 * 1024

F32 = jnp.float32
BF16 = jnp.bfloat16
TRANS_B = (((1,), (1,)), ((), ()))


def _params(sem):
    return pltpu.CompilerParams(dimension_semantics=sem, vmem_limit_bytes=VMEM_LIMIT)


def _row_tile(t, cap):
    r = min(cap, t)
    assert t % r == 0
    return r


def _rms(x, g):
    return x * lax.rsqrt(jnp.mean(x * x, axis=-1, keepdims=True) + EPS) * g


def _rmsnorm_kernel(x_ref, g_ref, o_ref):
    o_ref[...] = _rms(x_ref[...], g_ref[...]).astype(o_ref.dtype)


def _add_rmsnorm_kernel(x_ref, d_ref, g_ref, o_ref):
    o_ref[...] = _rms(x_ref[...] + d_ref[...], g_ref[...]).astype(o_ref.dtype)


def _add_rmsnorm_split_kernel(x_ref, d_ref, g_ref, a_ref, b_ref, *, n_first):
    y = _rms(x_ref[...] + d_ref[...], g_ref[...])
    i = pl.program_id(0)

    @pl.when(i < n_first)
    def _():
        a_ref[...] = y

    @pl.when(i >= n_first)
    def _():
        b_ref[...] = y


def _rmsnorm_pair_kernel(a_ref, b_ref, g_ref, o_ref, *, n_first):
    x = jnp.where(pl.program_id(0) < n_first, a_ref[...], b_ref[...])
    o_ref[...] = _rms(x, g_ref[...]).astype(o_ref.dtype)


def _pair_specs(tr, width, nf, col):
    return [pl.BlockSpec((tr, width), lambda i, *g: (jnp.minimum(i, nf - 1), jnp.where(i < nf, col(i, *g), 0))),
            pl.BlockSpec((tr, width), lambda i, *g: (jnp.maximum(i - nf, 0), jnp.where(i >= nf, col(i, *g), 0)))]


def rmsnorm_pair(xa, xb, gain):
    ta, d = xa.shape
    tb = xb.shape[0]
    tr = _row_tile(math.gcd(ta, tb), 256)
    nf = ta // tr
    return pl.pallas_call(
        functools.partial(_rmsnorm_pair_kernel, n_first=nf), grid=((ta + tb) // tr,),
        in_specs=_pair_specs(tr, d, nf, lambda i: 0) + [pl.BlockSpec((1, d), lambda i: (0, 0))],
        out_specs=pl.BlockSpec((tr, d), lambda i: (i, 0)),
        out_shape=jax.ShapeDtypeStruct((ta + tb, d), BF16),
        compiler_params=_params(("arbitrary",)), name="rmsnorm_pair")(xa, xb, gain.reshape(1, d))


def _cast_kernel(x_ref, o_ref):
    o_ref[...] = x_ref[...].astype(o_ref.dtype)


def cast_bf16(w):
    cols = w.shape[-1]
    w2 = w.reshape(-1, cols)
    rows = w2.shape[0]
    tr = _row_tile(rows, 512)
    tc = max(c for c in range(128, min(cols, 4096) + 1, 128) if cols % c == 0)
    blk = pl.BlockSpec((tr, tc), lambda i, j: (i, j))
    out = pl.pallas_call(
        _cast_kernel, grid=(rows // tr, cols // tc), in_specs=[blk], out_specs=blk,
        out_shape=jax.ShapeDtypeStruct((rows, cols), BF16),
        compiler_params=_params(("parallel", "parallel")), name="cast_bf16")(w2)
    return out.reshape(w.shape)


def rmsnorm(x, gain, out_dtype):
    t, d = x.shape
    tr = _row_tile(t, 256)
    row = pl.BlockSpec((tr, d), lambda i: (i, 0))
    return pl.pallas_call(
        _rmsnorm_kernel, grid=(t // tr,),
        in_specs=[row, pl.BlockSpec((1, d), lambda i: (0, 0))],
        out_specs=row, out_shape=jax.ShapeDtypeStruct((t, d), out_dtype),
        compiler_params=_params(("parallel",)), name="rmsnorm")(x, gain.reshape(1, d))


def add_rmsnorm(x, delta, gain):
    t, d = x.shape
    tr = _row_tile(t, 256)
    row = pl.BlockSpec((tr, d), lambda i: (i, 0))
    return pl.pallas_call(
        _add_rmsnorm_kernel, grid=(t // tr,),
        in_specs=[row, row, pl.BlockSpec((1, d), lambda i: (0, 0))],
        out_specs=row, out_shape=jax.ShapeDtypeStruct((t, d), BF16),
        compiler_params=_params(("parallel",)), name="add_rmsnorm")(x, delta, gain.reshape(1, d))


def add_rmsnorm_split(x, delta, gain, t_first):
    t, d = x.shape
    tr = _row_tile(math.gcd(t_first, t - t_first), 256)
    nf = t_first // tr
    row = pl.BlockSpec((tr, d), lambda i: (i, 0))
    return pl.pallas_call(
        functools.partial(_add_rmsnorm_split_kernel, n_first=nf), grid=(t // tr,),
        in_specs=[row, row, pl.BlockSpec((1, d), lambda i: (0, 0))],
        out_specs=[pl.BlockSpec((tr, d), lambda i: (jnp.minimum(i, nf - 1), 0)),
                   pl.BlockSpec((tr, d), lambda i: (jnp.maximum(i - nf, 0), 0))],
        out_shape=[jax.ShapeDtypeStruct((t_first, d), F32), jax.ShapeDtypeStruct((t - t_first, d), F32)],
        compiler_params=_params(("arbitrary",)), name="add_rmsnorm_split")(x, delta, gain.reshape(1, d))


PROJ_TN = 1024
PROJ_CHUNK = 256
_ROPE_TILES = 2 * ATT_PART // PROJ_TN
_RETROT_TILES = 2 * RET_QK_WIDTH // PROJ_TN
_RETK_TILE = RET_QK_WIDTH // PROJ_TN


def _proj_att_kernel(x_ref, w_ref, cos_ref, sin_ref, o_ref):
    rotate = pl.program_id(1) < _ROPE_TILES
    cos = cos_ref[...]
    sin = sin_ref[...]
    x = x_ref[...]
    for c0 in range(0, PROJ_TN, PROJ_CHUNK):
        acc = jnp.dot(x, w_ref[:, c0:c0 + PROJ_CHUNK], preferred_element_type=F32)
        for h0 in range(0, PROJ_CHUNK, HEAD_DIM):
            xs = acc[:, h0:h0 + HEAD_DIM]
            rot = xs * cos + pltpu.roll(xs, HEAD_DIM // 2, axis=1) * sin
            o_ref[:, c0 + h0:c0 + h0 + HEAD_DIM] = jnp.where(rotate, rot, xs).astype(o_ref.dtype)


def _proj_ret_kernel(x_ref, w_ref, cos_ref, sin_ref, o_ref):
    j = pl.program_id(1)
    rotate = j < _RETROT_TILES
    scale = jnp.where(j == _RETK_TILE, HEAD_DIM ** -0.5, 1.0).astype(F32)
    cos = cos_ref[...]
    sin = sin_ref[...]
    x = x_ref[...]
    even = (lax.broadcasted_iota(jnp.int32, (x.shape[0], HEAD_DIM), 1) % 2) == 0
    for c0 in range(0, PROJ_TN, PROJ_CHUNK):
        acc = jnp.dot(x, w_ref[:, c0:c0 + PROJ_CHUNK], preferred_element_type=F32)
        for h0 in range(0, PROJ_CHUNK, HEAD_DIM):
            xs = acc[:, h0:h0 + HEAD_DIM]
            sw = jnp.where(even, pltpu.roll(xs, HEAD_DIM - 1, axis=1), pltpu.roll(xs, 1, axis=1))
            rot = (xs * cos + sw * sin) * scale
            o_ref[:, c0 + h0:c0 + h0 + HEAD_DIM] = jnp.where(rotate, rot, xs).astype(o_ref.dtype)


def _rotation_tables(s):
    pos = jnp.arange(s, dtype=F32)[:, None]
    inv = ROPE_THETA ** (-jnp.arange(0, HEAD_DIM, 2, dtype=F32) / HEAD_DIM)
    ang = pos * inv[None, :]
    rc = jnp.concatenate([jnp.cos(ang), jnp.cos(ang)], axis=-1)
    rs = jnp.concatenate([-jnp.sin(ang), jnp.sin(ang)], axis=-1)
    inv_r = 1.0 / (RET_THETA ** jnp.linspace(0.0, 1.0, HEAD_DIM // 2, dtype=F32))
    ang_r = pos * inv_r[None, :]
    qc = jnp.repeat(jnp.cos(ang_r), 2, axis=-1)
    qs = jnp.stack([-jnp.sin(ang_r), jnp.sin(ang_r)], axis=-1).reshape(s, HEAD_DIM)
    return rc, rs, qc, qs


def input_projection(xn, w_in, layer, tables, s):
    t, d = xn.shape
    tm = _row_tile(s, 1024)
    nsb = s // tm
    tab = pl.BlockSpec((tm, HEAD_DIM), lambda i, j: (i % nsb, 0))
    xspec = pl.BlockSpec((tm, d), lambda i, j: (i, 0))
    ospec = pl.BlockSpec((tm, PROJ_TN), lambda i, j: (i, j))
    att = pl.pallas_call(
        _proj_att_kernel, grid=(t // tm, ATT_COLS // PROJ_TN),
        in_specs=[xspec, pl.BlockSpec((None, d, PROJ_TN), lambda i, j: (layer, 0, j)), tab, tab],
        out_specs=ospec, out_shape=jax.ShapeDtypeStruct((t, ATT_COLS), F32),
        compiler_params=_params(("parallel", "arbitrary")), name="proj_attention")(xn, w_in, *tables[:2])
    ret0 = ATT_COLS // PROJ_TN
    ret = pl.pallas_call(
        _proj_ret_kernel, grid=(t // tm, RET_COLS // PROJ_TN),
        in_specs=[xspec, pl.BlockSpec((None, d, PROJ_TN), lambda i, j: (layer, 0, ret0 + j)), tab, tab],
        out_specs=ospec, out_shape=jax.ShapeDtypeStruct((t, RET_COLS), BF16),
        compiler_params=_params(("parallel", "arbitrary")), name="proj_retention")(xn, w_in, *tables[2:])
    return att, ret


ATT_BLOCKS = 16
ATT_UNROLL = 16
ATT_PERM_ROWS = 256


def _rows(start, size, stride):
    return pl.ds(start, size) if stride == 1 else pl.ds(start, size, stride=stride)


def _att_kernel(q0, q1, q2, k0, k1, k2, v0, v1, v2, o_ref, og0, og1, og2, lg0, lg1, lg2, cls_ref, *, seq):
    w = ATT_HALF_WINDOW
    scale = HEAD_DIM ** -0.5
    groups = ((q0, k0, v0, og0, lg0), (q1, k1, v1, og1, lg1), (q2, k2, v2, og2, lg2))
    for (q_ref, k_ref, v_ref, og_ref, lg_ref), dil in zip(groups, ATT_DILATIONS):
        sub_len = seq // dil
        bq = min(128, sub_len)
        kw = min(bq + 2 * w, sub_len)
        nqb = sub_len // bq
        assert nqb * dil == ATT_BLOCKS
        rel = (lax.broadcasted_iota(jnp.int32, (bq, kw), 1)
               - lax.broadcasted_iota(jnp.int32, (bq, kw), 0))
        by_class = nqb == 1 and dil * dil <= ATT_PERM_ROWS and seq % ATT_PERM_ROWS == 0
        if by_class:
            per = ATT_PERM_ROWS // dil
            r = lax.broadcasted_iota(jnp.int32, (ATT_PERM_ROWS, ATT_PERM_ROWS), 0)
            c = lax.broadcasted_iota(jnp.int32, (ATT_PERM_ROWS, ATT_PERM_ROWS), 1)
            perm = jnp.where(c == (r % per) * dil + r // per, 1.0, 0.0).astype(BF16)
            for t0 in range(0, seq, ATT_PERM_ROWS):
                rows = slice(t0, t0 + ATT_PERM_ROWS)
                tile = jnp.concatenate([ref[rows, :].astype(BF16) for ref in (q_ref, k_ref, v_ref)], axis=1)
                moved = jnp.dot(perm, tile, preferred_element_type=F32).astype(BF16)
                for part in range(3):
                    cls_ref[part, rows, :] = moved[:, part * HEAD_DIM:(part + 1) * HEAD_DIM]

        def scores(it, q_ref=q_ref, k_ref=k_ref, v_ref=v_ref, dil=dil, sub_len=sub_len, bq=bq, kw=kw,
                   nqb=nqb, rel=rel, by_class=by_class):
            res = it // nqb
            qs = (it % nqb) * bq
            ks = jnp.clip(qs - w, 0, sub_len - kw)
            if by_class:
                per = ATT_PERM_ROWS // dil

                def gather(part):
                    return jnp.concatenate(
                        [cls_ref[part, pl.ds(pl.multiple_of(t0 + res * per, per), per), :]
                         for t0 in range(0, seq, ATT_PERM_ROWS)], axis=0)

                q, k, v = gather(0), gather(1), gather(2)
            else:
                q = q_ref[_rows(qs * dil + res, bq, dil), :].astype(BF16)
                k = k_ref[_rows(ks * dil + res, kw, dil), :].astype(BF16)
                v = v_ref[_rows(ks * dil + res, kw, dil), :].astype(BF16)
            s = lax.dot_general(q, k, TRANS_B, preferred_element_type=F32) * scale
            s = jnp.where(jnp.abs(rel + (ks - qs)) <= w, s, NEG_INF)
            return s, v, _rows(qs * dil + res, bq, dil)

        def blocks(it, carry, scores=scores, og_ref=og_ref, lg_ref=lg_ref, bq=bq):
            staged = [scores(it * ATT_UNROLL + u) for u in range(ATT_UNROLL)]
            probs = []
            for s, _, _ in staged:
                m = jnp.max(s, axis=-1, keepdims=True)
                e = jnp.exp(s - m)
                l = jnp.sum(e, axis=-1, keepdims=True)
                probs.append(((e * (1.0 / l)).astype(BF16), m + jnp.log(l)))
            for (p, lse), (_, v, rows) in zip(probs, staged):
                og_ref[rows, :] = jnp.dot(p, v, preferred_element_type=F32)
                lg_ref[rows, :] = jnp.broadcast_to(lse, (bq, HEAD_DIM))
            return carry

        lax.fori_loop(0, ATT_BLOCKS // ATT_UNROLL, blocks, 0)

    mb = min(256, seq)

    def merge(c, carry):
        rows = pl.ds(pl.multiple_of(c * mb, mb), mb)
        a, b, cc = lg0[rows, :], lg1[rows, :], lg2[rows, :]
        m = jnp.maximum(jnp.maximum(a, b), cc)
        ea, eb, ec = jnp.exp(a - m), jnp.exp(b - m), jnp.exp(cc - m)
        z = ea + eb + ec
        o_ref[rows, :] = ((ea / z) * og0[rows, :] + (eb / z) * og1[rows, :]
                          + (ec / z) * og2[rows, :]).astype(o_ref.dtype)
        return carry

    lax.fori_loop(0, seq // mb, merge, 0)


def dilated_attention(att, batch, s):
    t = batch * s

    def spec(part, group):
        base = (part * ATT_PART + group * ATT_GROUP_WIDTH) // HEAD_DIM
        return pl.BlockSpec((s, HEAD_DIM), lambda b, h: (b, base + h))

    ng = len(ATT_DILATIONS)
    return pl.pallas_call(
        functools.partial(_att_kernel, seq=s), grid=(batch, ATT_HEADS),
        in_specs=[spec(p, g) for p in range(3) for g in range(ng)],
        out_specs=pl.BlockSpec((s, HEAD_DIM), lambda b, h: (b, h)),
        out_shape=jax.ShapeDtypeStruct((t, ATT_GROUP_WIDTH), BF16),
        scratch_shapes=[pltpu.VMEM((s, HEAD_DIM), F32)] * (2 * ng) + [pltpu.VMEM((3, s, HEAD_DIM), BF16)],
        compiler_params=_params(("parallel", "parallel")), name="dilated_attention")(*([att] * (3 * ng)))


def _log_sigmoid(x):
    return jnp.minimum(x, 0.0) - jnp.log1p(jnp.exp(-jnp.abs(x)))


RET_STEP_HEADS = 4


def _retention_kernel(dec_ref, q_ref, k_ref, v_ref, g_ref, o_ref, acc_ref, accb_ref, *, seq):
    c = RET_CHUNK
    nc = seq // c
    hs = RET_STEP_HEADS
    h0 = pl.program_id(1) * hs
    row = lax.broadcasted_iota(jnp.int32, (c, RET_V_DIM), 0).astype(F32)
    row_k = lax.broadcasted_iota(jnp.int32, (c, HEAD_DIM), 0).astype(F32)
    rel = (lax.broadcasted_iota(jnp.int32, (c, c), 0) - lax.broadcasted_iota(jnp.int32, (c, c), 1)).astype(F32)
    consts = []
    for hh in range(hs):
        lgf = _log_sigmoid(jnp.full((1, 1), dec_ref[0, h0 + hh], F32))
        lgb = _log_sigmoid(jnp.full((1, 1), dec_ref[1, h0 + hh], F32))
        consts.append(dict(
            decay=jnp.where(rel >= 0, jnp.exp(jnp.maximum(rel, 0.0) * lgf), jnp.exp(jnp.maximum(-rel, 0.0) * lgb)),
            qdec_f=jnp.exp((row + 1.0) * lgf),
            qdec_b=jnp.exp((c - row) * lgb),
            kdec_f=jnp.exp((c - 1.0 - row_k) * lgf),
            kdec_b=jnp.exp(row_k * lgb),
            chunk_f=jnp.exp(c * lgf), chunk_b=jnp.exp(c * lgb)))

    def load(ci, hh):
        rows = pl.ds(pl.multiple_of(ci * c, c), c)
        qk = slice(hh * HEAD_DIM, (hh + 1) * HEAD_DIM)
        vv = slice(hh * RET_V_DIM, (hh + 1) * RET_V_DIM)
        return rows, vv, q_ref[rows, qk], k_ref[rows, qk], v_ref[rows, vv]

    def scans(step, states):
        out, pending = [], []
        for hh, (state_f, state_b) in enumerate(states):
            cn = consts[hh]
            rows, vv, q, k, v = load(step, hh)
            a = lax.dot_general(q, k, TRANS_B, preferred_element_type=F32)
            cross = jnp.dot(q, state_f.astype(BF16), preferred_element_type=F32) * cn["qdec_f"]
            kd = (k.astype(F32) * cn["kdec_f"]).T.astype(BF16)
            state_f = state_f * cn["chunk_f"] + jnp.dot(kd, v, preferred_element_type=F32)

            rows1, _, q1, k1, v1 = load(nc - 1 - step, hh)
            accb_ref[rows1, vv] = jnp.dot(q1, state_b.astype(BF16), preferred_element_type=F32) * cn["qdec_b"]
            kd1 = (k1.astype(F32) * cn["kdec_b"]).T.astype(BF16)
            state_b = state_b * cn["chunk_b"] + jnp.dot(kd1, v1, preferred_element_type=F32)
            out.append((state_f, state_b))
            pending.append((rows, vv, (a * cn["decay"]).astype(BF16), v, cross))
        for rows, vv, a_dec, v, cross in pending:
            acc_ref[rows, vv] = jnp.dot(a_dec, v, preferred_element_type=F32) + cross
        return tuple(out)

    zero = jnp.zeros((HEAD_DIM, RET_V_DIM), F32)
    lax.fori_loop(0, nc, scans, ((zero, zero),) * hs)

    def finish(ci, carry):
        rows = pl.ds(pl.multiple_of(ci * c, c), c)
        for hh in range(hs):
            vv = slice(hh * RET_V_DIM, (hh + 1) * RET_V_DIM)
            y = acc_ref[rows, vv] + accb_ref[rows, vv]
            y = y * lax.rsqrt(jnp.mean(y * y, axis=-1, keepdims=True) + EPS)
            g = g_ref[rows, vv].astype(F32)
            o_ref[rows, vv] = (g * jax.nn.sigmoid(g) * y).astype(o_ref.dtype)
        return carry

    lax.fori_loop(0, nc, finish, 0)


def retention(ret, decays, batch, s):
    t = batch * s
    hs = RET_STEP_HEADS
    qw, vw = hs * HEAD_DIM, hs * RET_V_DIM
    qk = lambda base: pl.BlockSpec((s, qw), lambda b, h: (b, base // qw + h))
    vg = lambda base: pl.BlockSpec((s, vw), lambda b, h: (b, base // vw + h))
    return pl.pallas_call(
        functools.partial(_retention_kernel, seq=s),
        grid=(batch, RET_HEADS // hs),
        in_specs=[pl.BlockSpec(memory_space=pltpu.SMEM),
                  qk(0), qk(RET_QK_WIDTH), vg(2 * RET_QK_WIDTH), vg(2 * RET_QK_WIDTH + RET_V_WIDTH)],
        out_specs=pl.BlockSpec((s, vw), lambda b, h: (b, h)),
        out_shape=jax.ShapeDtypeStruct((t, RET_V_WIDTH), BF16),
        scratch_shapes=[pltpu.VMEM((s, vw), F32)] * 2,
        compiler_params=_params(("parallel", "parallel")), name="retention")(decays, ret, ret, ret, ret)


def _branch_merge_kernel(xn_ref, oa_ref, ob_ref, wga_ref, wgr_ref, wa_ref, wr_ref, ba_ref, br_ref, o_ref):
    xn = xn_ref[...]
    ga = jax.nn.sigmoid(jnp.dot(xn, wga_ref[...], preferred_element_type=F32) + ba_ref[...])
    gr = jax.nn.sigmoid(jnp.dot(xn, wgr_ref[...], preferred_element_type=F32) + br_ref[...])
    a = jnp.dot(oa_ref[...], wa_ref[...], preferred_element_type=F32)
    r = jnp.dot(ob_ref[...], wr_ref[...], preferred_element_type=F32)
    o_ref[...] = (ga * a + gr * r).astype(o_ref.dtype)


def branch_merge(xn, o_a, o_b, w_gate, b_gate, w_a, w_r, layer):
    t, d = xn.shape
    tm = _row_tile(t, 512)
    tn = min(512, d)
    nb = d // tn
    res = lambda width: pl.BlockSpec((tm, width), lambda i, n: (i, 0))
    return pl.pallas_call(
        _branch_merge_kernel, grid=(t // tm, nb),
        in_specs=[res(d), res(o_a.shape[1]), res(o_b.shape[1]),
                  pl.BlockSpec((None, d, tn), lambda i, n: (layer, 0, n)),
                  pl.BlockSpec((None, d, tn), lambda i, n: (layer, 0, nb + n)),
                  pl.BlockSpec((None, o_a.shape[1], tn), lambda i, n: (layer, 0, n)),
                  pl.BlockSpec((None, o_b.shape[1], tn), lambda i, n: (layer, 0, n)),
                  pl.BlockSpec((None, 1, tn), lambda i, n: (layer, 0, n)),
                  pl.BlockSpec((None, 1, tn), lambda i, n: (layer, 0, nb + n))],
        out_specs=pl.BlockSpec((tm, tn), lambda i, n: (i, n)),
        out_shape=jax.ShapeDtypeStruct((t, d), BF16),
        compiler_params=_params(("parallel", "arbitrary")), name="branch_merge")(
            xn, o_a, o_b, w_gate, w_gate, w_a, w_r, b_gate, b_gate)


def _residual_pair_kernel(a_ref, b_ref, m_ref, w_ref, o_ref, *, n_first):
    x = jnp.where(pl.program_id(0) < n_first, a_ref[...], b_ref[...])
    o_ref[...] = x + jnp.dot(m_ref[...], w_ref[...], preferred_element_type=F32)


def _residual_sum_kernel(x_ref, d_ref, m_ref, w_ref, o_ref):
    o_ref[...] = (x_ref[...] + d_ref[...]) + jnp.dot(m_ref[...], w_ref[...], preferred_element_type=F32)


def residual_matmul(x_parts, merged, w_out, layer, stacked):
    xa, xb = x_parts
    t, d = merged.shape
    tm = _row_tile(math.gcd(xa.shape[0], t), 1024)
    tn = min(512, d)
    tile = pl.BlockSpec((tm, tn), lambda i, n: (i, n))
    if stacked:
        nf = xa.shape[0] // tm
        body = functools.partial(_residual_pair_kernel, n_first=nf)
        x_specs = _pair_specs(tm, tn, nf, lambda i, n: n)
    else:
        body = _residual_sum_kernel
        x_specs = [tile, tile]
    return pl.pallas_call(
        body, grid=(t // tm, d // tn),
        in_specs=x_specs + [pl.BlockSpec((tm, d), lambda i, n: (i, 0)),
                            pl.BlockSpec((None, d, tn), lambda i, n: (layer, 0, n))],
        out_specs=tile, out_shape=jax.ShapeDtypeStruct((t, d), F32),
        compiler_params=_params(("arbitrary", "arbitrary")), name="residual_matmul")(xa, xb, merged, w_out)


PEER_SEL_TM = 128
PEER_SEL_HEADS = 4
PEER_W_ROWS = 8
_BIG_INDEX = float(1 << 20)
_NSEL = PEER_HEADS * PEER_TOPK

_PAIR_CANDIDATES = [(a, b) for a in range(PEER_TOPK) for b in range(PEER_TOPK)
                    if (a + 1) * (b + 1) <= PEER_TOPK]


def _peer_query_kernel(x_ref, w_ref, o_ref):
    acc = jnp.dot(x_ref[...], w_ref[...], preferred_element_type=F32).astype(o_ref.dtype)
    for c in range(o_ref.shape[0]):
        o_ref[c] = acc[:, c * HEAD_DIM:(c + 1) * HEAD_DIM]


def peer_query(x, w, layer):
    t, d = x.shape
    n = w.shape[2]
    tm = _row_tile(t, 1024)
    tn = min(1024, n)
    return pl.pallas_call(
        _peer_query_kernel, grid=(t // tm, n // tn),
        in_specs=[pl.BlockSpec((tm, d), lambda i, j: (i, 0)),
                  pl.BlockSpec((None, d, tn), lambda i, j: (layer, 0, j))],
        out_specs=pl.BlockSpec((tn // HEAD_DIM, tm, HEAD_DIM), lambda i, j: (j, i, 0)),
        out_shape=jax.ShapeDtypeStruct((n // HEAD_DIM, t, HEAD_DIM), BF16),
        compiler_params=_params(("parallel", "arbitrary")), name="peer_query")(x, w)


def _topk_rows(x, idx, k):
    vals, idxs = [], []
    for _ in range(k):
        m = jnp.max(x, axis=0, keepdims=True)
        sel = jnp.min(jnp.where(x == m, idx, _BIG_INDEX), axis=0, keepdims=True)
        vals.append(m)
        idxs.append(sel)
        x = jnp.where(idx == sel, -jnp.inf, x)
    return vals, idxs


def _peer_select_kernel(q_ref, keys_ref, w_ref, sel_t, sel_rows):
    step = pl.program_id(0)
    tm = q_ref.shape[1]
    k = PEER_TOPK
    slot = step % 2
    prev = 1 - slot
    per_iter = tm * PEER_SEL_HEADS // PEER_HEADS

    @pl.when(step == 0)
    def _():
        sel_rows[1] = jnp.zeros(sel_rows.shape[1:], F32)

    key_idx = lax.broadcasted_iota(jnp.int32, (N_KEYS, tm), 0).astype(F32)
    ncand = len(_PAIR_CANDIDATES)
    npad = -(-ncand // 8) * 8
    cand_row = lax.broadcasted_iota(jnp.int32, (npad, tm), 0)
    cand_idx = jnp.full((npad, tm), _BIG_INDEX, F32)
    for r, (a, b) in enumerate(_PAIR_CANDIDATES):
        cand_idx = jnp.where(cand_row == r, float(a * k + b), cand_idx)
    pad_rows = [jnp.full((1, tm), -jnp.inf, F32)] * (npad - ncand)
    sub = lax.broadcasted_iota(jnp.int32, (N_KEYS, _NSEL), 0).astype(F32)

    def select_head(h):
        tops = []
        for p in range(2):
            sc = lax.dot_general(keys_ref[h, p], q_ref[2 * h + p], TRANS_B, preferred_element_type=F32)
            tops.append(_topk_rows(sc, key_idx, k))
        (v0, i0), (v1, i1) = tops
        cand = jnp.concatenate([v0[a] + v1[b] for a, b in _PAIR_CANDIDATES] + pad_rows, axis=0)
        sel_s, sel_pos = _topk_rows(cand, cand_idx, k)
        sel_s = jnp.concatenate(sel_s, axis=0)
        sel_pos = jnp.concatenate(sel_pos, axis=0)
        a_sel = jnp.floor(sel_pos * (1.0 / k))
        b_sel = sel_pos - a_sel * k
        i_sel = jnp.zeros((k, tm), F32)
        j_sel = jnp.zeros((k, tm), F32)
        for r in range(k):
            i_sel = i_sel + jnp.where(a_sel == r, i0[r], 0.0)
            j_sel = j_sel + jnp.where(b_sel == r, i1[r], 0.0)
        e = jnp.exp(sel_s - sel_s[0:1, :])
        rows = pl.ds(pl.multiple_of(h * k, k), k)
        sel_t[0, rows, :] = i_sel
        sel_t[1, rows, :] = j_sel
        sel_t[2, rows, :] = e / jnp.sum(e, axis=0, keepdims=True)

    def build_mask(tk):
        row = pl.ds(tk, 1)
        eq_i = sub == sel_rows[prev, 0, row, :]
        eq_j = sub == sel_rows[prev, 1, row, :]
        lhs = jnp.concatenate([jnp.where(eq_i, sel_rows[prev, 2, row, :], 0.0),
                               jnp.where(eq_i, sel_rows[prev, 3, row, :], 0.0)], axis=1).astype(BF16)
        one = jnp.where(eq_j, 1.0, 0.0).astype(BF16)
        rhs = jnp.concatenate([one, one], axis=1)
        wt = lax.dot_general(lhs, rhs, TRANS_B, preferred_element_type=F32)
        w_ref[:, tk] = wt.reshape(N_KEYS // PEER_W_ROWS, PEER_W_ROWS, N_KEYS)

    def heads(it, carry):
        for hh in range(PEER_SEL_HEADS):
            select_head(it * PEER_SEL_HEADS + hh)
        for u in range(per_iter):
            build_mask(it * per_iter + u)
        return carry

    lax.fori_loop(0, PEER_HEADS // PEER_SEL_HEADS, heads, 0)

    gate = sel_t[2].T
    gate_hi = gate.astype(BF16).astype(F32)
    sel_rows[slot, 0] = sel_t[0].T
    sel_rows[slot, 1] = sel_t[1].T
    sel_rows[slot, 2] = gate_hi
    sel_rows[slot, 3] = gate - gate_hi


def peer_select(q, sub_keys, layer):
    nq, t, _ = q.shape
    tm = _row_tile(t, PEER_SEL_TM)
    nt = t // tm
    nib = N_KEYS // PEER_W_ROWS
    return pl.pallas_call(
        _peer_select_kernel, grid=(nt + 1,),
        in_specs=[pl.BlockSpec((nq, tm, HEAD_DIM), lambda s: (0, jnp.minimum(s, nt - 1), 0)),
                  pl.BlockSpec((None,) + sub_keys.shape[1:], lambda s: (layer, 0, 0, 0, 0))],
        out_specs=pl.BlockSpec((nib, tm, PEER_W_ROWS, N_KEYS), lambda s: (0, jnp.maximum(s - 1, 0), 0, 0)),
        out_shape=jax.ShapeDtypeStruct((nib, t, PEER_W_ROWS, N_KEYS), F32),
        scratch_shapes=[pltpu.VMEM((3, _NSEL, tm), F32), pltpu.VMEM((2, 4, tm, _NSEL), F32)],
        compiler_params=_params(("arbitrary",)), name="peer_select")(q, sub_keys)


PEER_TE = 512
PEER_TM = 1024


def _gelu(x):
    return 0.5 * x * (1.0 + lax.erf(x * math.sqrt(0.5)))


def _peer_experts_kernel(xn_ref, u_ref, v_ref, w_ref, o_ref, mix_scr, *, ne):
    s = pl.program_id(0)
    tm = xn_ref.shape[0]
    sub_blocks = PEER_TE // N_KEYS
    slot = s % 2
    e = s % ne

    @pl.when(s == 0)
    def _():
        o_ref[...] = jnp.zeros_like(o_ref)
        mix_scr[1] = jnp.zeros(mix_scr.shape[1:], mix_scr.dtype)

    starts_tile = jnp.logical_or(s == 0, (s - 1) % ne == 0)
    prev = jnp.where(starts_tile, 0.0, o_ref[...])
    o_ref[...] = prev + jnp.dot(mix_scr[1 - slot], v_ref[...], preferred_element_type=F32)

    h = lax.dot_general(xn_ref[...], u_ref[...], TRANS_B, preferred_element_type=F32)
    base = (e % (PEER_W_ROWS // sub_blocks)) * sub_blocks
    parts = []
    for ii in range(sub_blocks):
        wsub = w_ref[pl.ds(base + ii, tm, stride=PEER_W_ROWS), :]
        parts.append((wsub * _gelu(h[:, ii * N_KEYS:(ii + 1) * N_KEYS])).astype(BF16))
    mix_scr[slot] = jnp.concatenate(parts, axis=1)


def peer_experts(xn, u, v, w, layer):
    t, d = xn.shape
    ne = u.shape[1] // PEER_TE
    tm = _row_tile(t, PEER_TM)
    per_w = PEER_W_ROWS * N_KEYS // PEER_TE
    nt = t // tm
    steps = nt * ne
    w2 = w.reshape(-1, N_KEYS)
    cur = lambda s: jnp.minimum(s, steps - 1)
    last = lambda s: jnp.maximum(s - 1, 0)
    once = pl.Buffered(1)
    return pl.pallas_call(
        functools.partial(_peer_experts_kernel, ne=ne), grid=(steps + 1,),
        in_specs=[pl.BlockSpec((tm, d), lambda s: (cur(s) // ne, 0), pipeline_mode=once),
                  pl.BlockSpec((None, PEER_TE, d), lambda s: (layer, cur(s) % ne, 0)),
                  pl.BlockSpec((None, PEER_TE, d), lambda s: (layer, last(s) % ne, 0)),
                  pl.BlockSpec((tm * PEER_W_ROWS, N_KEYS),
                               lambda s: (((cur(s) % ne) // per_w) * nt + cur(s) // ne, 0))],
        out_specs=pl.BlockSpec((tm, d), lambda s: (last(s) // ne, 0), pipeline_mode=once),
        out_shape=jax.ShapeDtypeStruct((t, d), F32),
        scratch_shapes=[pltpu.VMEM((2, tm, PEER_TE), BF16)],
        compiler_params=_params(("arbitrary",)), name="peer_experts")(xn, u, v, w2)


def kernel(x_prompt, x_sample, norm_mix, w_in, ret_decay_fwd, ret_decay_bwd, w_branch_attn, w_branch_ret,
           w_gate, b_gate, w_out, norm_ffn, peer_w_q, peer_sub_keys, peer_u, peer_v, final_norm):
    assert x_prompt.shape[1:] == x_sample.shape[1:]
    bp, s, d = x_prompt.shape
    bs = x_sample.shape[0]
    batch = bp + bs
    t = batch * s
    depth = w_in.shape[0]
    assert w_in.shape[2] == IN_COLS and s % (ATT_DILATIONS[-1] * 2 * ATT_HALF_WINDOW) == 0

    w_in, w_gate, w_out, w_a, w_r, w_q, u, v = (
        cast_bf16(w) for w in (w_in, w_gate, w_out, w_branch_attn, w_branch_ret, peer_w_q, peer_u, peer_v))
    sub_keys = peer_sub_keys.astype(BF16)
    b_gate = b_gate.astype(F32).reshape(depth, 1, -1)
    tables = _rotation_tables(s)
    x_parts = (x_prompt.reshape(bp * s, d), x_sample.reshape(bs * s, d))
    for l in range(depth):
        if l == 0:
            xn = rmsnorm_pair(*x_parts, norm_mix[l])
        else:
            xn = add_rmsnorm(*x_parts, norm_mix[l])
        att, ret = input_projection(xn, w_in, l, tables, s)
        o_a = dilated_attention(att, batch, s)
        decays = jnp.stack([ret_decay_fwd[l], ret_decay_bwd[l]]).astype(F32)
        o_b = retention(ret, decays, batch, s)
        merged = branch_merge(xn, o_a, o_b, w_gate, b_gate, w_a, w_r, l)
        x = residual_matmul(x_parts, merged, w_out, l, stacked=(l == 0))
        xn2 = rmsnorm(x, norm_ffn[l], BF16)
        q = peer_query(xn2, w_q, l)
        w = peer_select(q, sub_keys, l)
        x_parts = (x, peer_experts(xn2, u, v, w, l))
    y_p, y_s = add_rmsnorm_split(*x_parts, final_norm, bp * s)
    return y_p.reshape(bp, s, d), y_s.reshape(bs, s, d)
```

```python
import functools
import math

import jax
import jax.numpy as jnp
from jax import lax
from jax.experimental import pallas as pl
from jax.experimental.pallas import tpu as pltpu

HEAD_DIM = 128
ATT_HEADS = 8
ATT_DILATIONS = (1, 4, 16)
ATT_HALF_WINDOW = 64
ATT_GROUP_WIDTH = ATT_HEADS * HEAD_DIM
ATT_PART = len(ATT_DILATIONS) * ATT_GROUP_WIDTH
ATT_COLS = 3 * ATT_PART
RET_HEADS = 8
RET_V_DIM = 256
RET_CHUNK = 128
RET_QK_WIDTH = RET_HEADS * HEAD_DIM
RET_V_WIDTH = RET_HEADS * RET_V_DIM
RET_COLS = 2 * RET_QK_WIDTH + 2 * RET_V_WIDTH
IN_COLS = ATT_COLS + RET_COLS
PEER_HEADS = 8
N_KEYS = 128
PEER_TOPK = 16
ROPE_THETA = 10000.0
RET_THETA = 10000.0
EPS = 1e-6
NEG_INF = -1e30

VMEM_LIMIT = 56 * 1024 * 1024

F32 = jnp.float32
BF16 = jnp.bfloat16
TRANS_B = (((1,), (1,)), ((), ()))


def _params(sem):
    return pltpu.CompilerParams(dimension_semantics=sem, vmem_limit_bytes=VMEM_LIMIT)


def _row_tile(t, cap):
    r = min(cap, t)
    assert t % r == 0
    return r


def _rms(x, g):
    return x * lax.rsqrt(jnp.mean(x * x, axis=-1, keepdims=True) + EPS) * g


def _rmsnorm_kernel(x_ref, g_ref, o_ref):
    o_ref[...] = _rms(x_ref[...], g_ref[...]).astype(o_ref.dtype)


def _add_rmsnorm_kernel(x_ref, d_ref, g_ref, o_ref):
    o_ref[...] = _rms(x_ref[...] + d_ref[...], g_ref[...]).astype(o_ref.dtype)


def _add_rmsnorm_split_kernel(x_ref, d_ref, g_ref, a_ref, b_ref, *, n_first):
    y = _rms(x_ref[...] + d_ref[...], g_ref[...])
    i = pl.program_id(0)

    @pl.when(i < n_first)
    def _():
        a_ref[...] = y

    @pl.when(i >= n_first)
    def _():
        b_ref[...] = y


def _rmsnorm_pair_kernel(a_ref, b_ref, g_ref, o_ref, *, n_first):
    x = jnp.where(pl.program_id(0) < n_first, a_ref[...], b_ref[...])
    o_ref[...] = _rms(x, g_ref[...]).astype(o_ref.dtype)


def _pair_specs(tr, width, nf, col):
    return [pl.BlockSpec((tr, width), lambda i, *g: (jnp.minimum(i, nf - 1), jnp.where(i < nf, col(i, *g), 0))),
            pl.BlockSpec((tr, width), lambda i, *g: (jnp.maximum(i - nf, 0), jnp.where(i >= nf, col(i, *g), 0)))]


def rmsnorm_pair(xa, xb, gain):
    ta, d = xa.shape
    tb = xb.shape[0]
    tr = _row_tile(math.gcd(ta, tb), 256)
    nf = ta // tr
    return pl.pallas_call(
        functools.partial(_rmsnorm_pair_kernel, n_first=nf), grid=((ta + tb) // tr,),
        in_specs=_pair_specs(tr, d, nf, lambda i: 0) + [pl.BlockSpec((1, d), lambda i: (0, 0))],
        out_specs=pl.BlockSpec((tr, d), lambda i: (i, 0)),
        out_shape=jax.ShapeDtypeStruct((ta + tb, d), BF16),
        compiler_params=_params(("arbitrary",)), name="rmsnorm_pair")(xa, xb, gain.reshape(1, d))


def _cast_kernel(x_ref, o_ref):
    o_ref[...] = x_ref[...].astype(o_ref.dtype)


def cast_bf16(w):
    cols = w.shape[-1]
    w2 = w.reshape(-1, cols)
    rows = w2.shape[0]
    tr = _row_tile(rows, 512)
    tc = max(c for c in range(128, min(cols, 4096) + 1, 128) if cols % c == 0)
    blk = pl.BlockSpec((tr, tc), lambda i, j: (i, j))
    out = pl.pallas_call(
        _cast_kernel, grid=(rows // tr, cols // tc), in_specs=[blk], out_specs=blk,
        out_shape=jax.ShapeDtypeStruct((rows, cols), BF16),
        compiler_params=_params(("parallel", "parallel")), name="cast_bf16")(w2)
    return out.reshape(w.shape)


def rmsnorm(x, gain, out_dtype):
    t, d = x.shape
    tr = _row_tile(t, 256)
    row = pl.BlockSpec((tr, d), lambda i: (i, 0))
    return pl.pallas_call(
        _rmsnorm_kernel, grid=(t // tr,),
        in_specs=[row, pl.BlockSpec((1, d), lambda i: (0, 0))],
        out_specs=row, out_shape=jax.ShapeDtypeStruct((t, d), out_dtype),
        compiler_params=_params(("parallel",)), name="rmsnorm")(x, gain.reshape(1, d))


def add_rmsnorm(x, delta, gain):
    t, d = x.shape
    tr = _row_tile(t, 256)
    row = pl.BlockSpec((tr, d), lambda i: (i, 0))
    return pl.pallas_call(
        _add_rmsnorm_kernel, grid=(t // tr,),
        in_specs=[row, row, pl.BlockSpec((1, d), lambda i: (0, 0))],
        out_specs=row, out_shape=jax.ShapeDtypeStruct((t, d), BF16),
        compiler_params=_params(("parallel",)), name="add_rmsnorm")(x, delta, gain.reshape(1, d))


def add_rmsnorm_split(x, delta, gain, t_first):
    t, d = x.shape
    tr = _row_tile(math.gcd(t_first, t - t_first), 256)
    nf = t_first // tr
    row = pl.BlockSpec((tr, d), lambda i: (i, 0))
    return pl.pallas_call(
        functools.partial(_add_rmsnorm_split_kernel, n_first=nf), grid=(t // tr,),
        in_specs=[row, row, pl.BlockSpec((1, d), lambda i: (0, 0))],
        out_specs=[pl.BlockSpec((tr, d), lambda i: (jnp.minimum(i, nf - 1), 0)),
                   pl.BlockSpec((tr, d), lambda i: (jnp.maximum(i - nf, 0), 0))],
        out_shape=[jax.ShapeDtypeStruct((t_first, d), F32), jax.ShapeDtypeStruct((t - t_first, d), F32)],
        compiler_params=_params(("arbitrary",)), name="add_rmsnorm_split")(x, delta, gain.reshape(1, d))


PROJ_TN = 1024
PROJ_CHUNK = 256
_ROPE_TILES = 2 * ATT_PART // PROJ_TN
_RETROT_TILES = 2 * RET_QK_WIDTH // PROJ_TN
_RETK_TILE = RET_QK_WIDTH // PROJ_TN


def _proj_att_kernel(x_ref, w_ref, cos_ref, sin_ref, o_ref):
    rotate = pl.program_id(1) < _ROPE_TILES
    cos = cos_ref[...]
    sin = sin_ref[...]
    x = x_ref[...]
    for c0 in range(0, PROJ_TN, PROJ_CHUNK):
        acc = jnp.dot(x, w_ref[:, c0:c0 + PROJ_CHUNK], preferred_element_type=F32)
        for h0 in range(0, PROJ_CHUNK, HEAD_DIM):
            xs = acc[:, h0:h0 + HEAD_DIM]
            rot = xs * cos + pltpu.roll(xs, HEAD_DIM // 2, axis=1) * sin
            o_ref[:, c0 + h0:c0 + h0 + HEAD_DIM] = jnp.where(rotate, rot, xs).astype(o_ref.dtype)


def _proj_ret_kernel(x_ref, w_ref, cos_ref, sin_ref, o_ref):
    j = pl.program_id(1)
    rotate = j < _RETROT_TILES
    scale = jnp.where(j == _RETK_TILE, HEAD_DIM ** -0.5, 1.0).astype(F32)
    cos = cos_ref[...]
    sin = sin_ref[...]
    x = x_ref[...]
    even = (lax.broadcasted_iota(jnp.int32, (x.shape[0], HEAD_DIM), 1) % 2) == 0
    for c0 in range(0, PROJ_TN, PROJ_CHUNK):
        acc = jnp.dot(x, w_ref[:, c0:c0 + PROJ_CHUNK], preferred_element_type=F32)
        for h0 in range(0, PROJ_CHUNK, HEAD_DIM):
            xs = acc[:, h0:h0 + HEAD_DIM]
            sw = jnp.where(even, pltpu.roll(xs, HEAD_DIM - 1, axis=1), pltpu.roll(xs, 1, axis=1))
            rot = (xs * cos + sw * sin) * scale
            o_ref[:, c0 + h0:c0 + h0 + HEAD_DIM] = jnp.where(rotate, rot, xs).astype(o_ref.dtype)


def _rotation_tables(s):
    pos = jnp.arange(s, dtype=F32)[:, None]
    inv = ROPE_THETA ** (-jnp.arange(0, HEAD_DIM, 2, dtype=F32) / HEAD_DIM)
    ang = pos * inv[None, :]
    rc = jnp.concatenate([jnp.cos(ang), jnp.cos(ang)], axis=-1)
    rs = jnp.concatenate([-jnp.sin(ang), jnp.sin(ang)], axis=-1)
    inv_r = 1.0 / (RET_THETA ** jnp.linspace(0.0, 1.0, HEAD_DIM // 2, dtype=F32))
    ang_r = pos * inv_r[None, :]
    qc = jnp.repeat(jnp.cos(ang_r), 2, axis=-1)
    qs = jnp.stack([-jnp.sin(ang_r), jnp.sin(ang_r)], axis=-1).reshape(s, HEAD_DIM)
    return rc, rs, qc, qs


def input_projection(xn, w_in, layer, tables, s):
    t, d = xn.shape
    tm = _row_tile(s, 1024)
    nsb = s // tm
    tab = pl.BlockSpec((tm, HEAD_DIM), lambda i, j: (i % nsb, 0))
    xspec = pl.BlockSpec((tm, d), lambda i, j: (i, 0))
    ospec = pl.BlockSpec((tm, PROJ_TN), lambda i, j: (i, j))
    att = pl.pallas_call(
        _proj_att_kernel, grid=(t // tm, ATT_COLS // PROJ_TN),
        in_specs=[xspec, pl.BlockSpec((None, d, PROJ_TN), lambda i, j: (layer, 0, j)), tab, tab],
        out_specs=ospec, out_shape=jax.ShapeDtypeStruct((t, ATT_COLS), F32),
        compiler_params=_params(("parallel", "arbitrary")), name="proj_attention")(xn, w_in, *tables[:2])
    ret0 = ATT_COLS // PROJ_TN
    ret = pl.pallas_call(
        _proj_ret_kernel, grid=(t // tm, RET_COLS // PROJ_TN),
        in_specs=[xspec, pl.BlockSpec((None, d, PROJ_TN), lambda i, j: (layer, 0, ret0 + j)), tab, tab],
        out_specs=ospec, out_shape=jax.ShapeDtypeStruct((t, RET_COLS), BF16),
        compiler_params=_params(("parallel", "arbitrary")), name="proj_retention")(xn, w_in, *tables[2:])
    return att, ret


ATT_BLOCKS = 16
ATT_UNROLL = 16
ATT_PERM_ROWS = 256


def _rows(start, size, stride):
    return pl.ds(start, size) if stride == 1 else pl.ds(start, size, stride=stride)


def _att_kernel(q0, q1, q2, k0, k1, k2, v0, v1, v2, o_ref, og0, og1, og2, lg0, lg1, lg2, cls_ref, *, seq):
    w = ATT_HALF_WINDOW
    scale = HEAD_DIM ** -0.5
    groups = ((q0, k0, v0, og0, lg0), (q1, k1, v1, og1, lg1), (q2, k2, v2, og2, lg2))
    for (q_ref, k_ref, v_ref, og_ref, lg_ref), dil in zip(groups, ATT_DILATIONS):
        sub_len = seq // dil
        bq = min(128, sub_len)
        kw = min(bq + 2 * w, sub_len)
        nqb = sub_len // bq
        assert nqb * dil == ATT_BLOCKS
        rel = (lax.broadcasted_iota(jnp.int32, (bq, kw), 1)
               - lax.broadcasted_iota(jnp.int32, (bq, kw), 0))
        by_class = nqb == 1 and dil * dil <= ATT_PERM_ROWS and seq % ATT_PERM_ROWS == 0
        if by_class:
            per = ATT_PERM_ROWS // dil
            r = lax.broadcasted_iota(jnp.int32, (ATT_PERM_ROWS, ATT_PERM_ROWS), 0)
            c = lax.broadcasted_iota(jnp.int32, (ATT_PERM_ROWS, ATT_PERM_ROWS), 1)
            perm = jnp.where(c == (r % per) * dil + r // per, 1.0, 0.0).astype(BF16)
            for t0 in range(0, seq, ATT_PERM_ROWS):
                rows = slice(t0, t0 + ATT_PERM_ROWS)
                tile = jnp.concatenate([ref[rows, :].astype(BF16) for ref in (q_ref, k_ref, v_ref)], axis=1)
                moved = jnp.dot(perm, tile, preferred_element_type=F32).astype(BF16)
                for part in range(3):
                    cls_ref[part, rows, :] = moved[:, part * HEAD_DIM:(part + 1) * HEAD_DIM]

        def scores(it, q_ref=q_ref, k_ref=k_ref, v_ref=v_ref, dil=dil, sub_len=sub_len, bq=bq, kw=kw,
                   nqb=nqb, rel=rel, by_class=by_class):
            res = it // nqb
            qs = (it % nqb) * bq
            ks = jnp.clip(qs - w, 0, sub_len - kw)
            if by_class:
                per = ATT_PERM_ROWS // dil

                def gather(part):
                    return jnp.concatenate(
                        [cls_ref[part, pl.ds(pl.multiple_of(t0 + res * per, per), per), :]
                         for t0 in range(0, seq, ATT_PERM_ROWS)], axis=0)

                q, k, v = gather(0), gather(1), gather(2)
            else:
                q = q_ref[_rows(qs * dil + res, bq, dil), :].astype(BF16)
                k = k_ref[_rows(ks * dil + res, kw, dil), :].astype(BF16)
                v = v_ref[_rows(ks * dil + res, kw, dil), :].astype(BF16)
            s = lax.dot_general(q, k, TRANS_B, preferred_element_type=F32) * scale
            s = jnp.where(jnp.abs(rel + (ks - qs)) <= w, s, NEG_INF)
            return s, v, _rows(qs * dil + res, bq, dil)

        def blocks(it, carry, scores=scores, og_ref=og_ref, lg_ref=lg_ref, bq=bq):
            staged = [scores(it * ATT_UNROLL + u) for u in range(ATT_UNROLL)]
            probs = []
            for s, _, _ in staged:
                m = jnp.max(s, axis=-1, keepdims=True)
                e = jnp.exp(s - m)
                l = jnp.sum(e, axis=-1, keepdims=True)
                probs.append(((e * (1.0 / l)).astype(BF16), m + jnp.log(l)))
            for (p, lse), (_, v, rows) in zip(probs, staged):
                og_ref[rows, :] = jnp.dot(p, v, preferred_element_type=F32)
                lg_ref[rows, :] = jnp.broadcast_to(lse, (bq, HEAD_DIM))
            return carry

        lax.fori_loop(0, ATT_BLOCKS // ATT_UNROLL, blocks, 0)

    mb = min(256, seq)

    def merge(c, carry):
        rows = pl.ds(pl.multiple_of(c * mb, mb), mb)
        a, b, cc = lg0[rows, :], lg1[rows, :], lg2[rows, :]
        m = jnp.maximum(jnp.maximum(a, b), cc)
        ea, eb, ec = jnp.exp(a - m), jnp.exp(b - m), jnp.exp(cc - m)
        z = ea + eb + ec
        o_ref[rows, :] = ((ea / z) * og0[rows, :] + (eb / z) * og1[rows, :]
                          + (ec / z) * og2[rows, :]).astype(o_ref.dtype)
        return carry

    lax.fori_loop(0, seq // mb, merge, 0)


def dilated_attention(att, batch, s):
    t = batch * s

    def spec(part, group):
        base = (part * ATT_PART + group * ATT_GROUP_WIDTH) // HEAD_DIM
        return pl.BlockSpec((s, HEAD_DIM), lambda b, h: (b, base + h))

    ng = len(ATT_DILATIONS)
    return pl.pallas_call(
        functools.partial(_att_kernel, seq=s), grid=(batch, ATT_HEADS),
        in_specs=[spec(p, g) for p in range(3) for g in range(ng)],
        out_specs=pl.BlockSpec((s, HEAD_DIM), lambda b, h: (b, h)),
        out_shape=jax.ShapeDtypeStruct((t, ATT_GROUP_WIDTH), BF16),
        scratch_shapes=[pltpu.VMEM((s, HEAD_DIM), F32)] * (2 * ng) + [pltpu.VMEM((3, s, HEAD_DIM), BF16)],
        compiler_params=_params(("parallel", "parallel")), name="dilated_attention")(*([att] * (3 * ng)))


def _log_sigmoid(x):
    return jnp.minimum(x, 0.0) - jnp.log1p(jnp.exp(-jnp.abs(x)))


RET_STEP_HEADS = 4


def _retention_kernel(dec_ref, q_ref, k_ref, v_ref, g_ref, o_ref, acc_ref, accb_ref, *, seq):
    c = RET_CHUNK
    nc = seq // c
    hs = RET_STEP_HEADS
    h0 = pl.program_id(1) * hs
    row = lax.broadcasted_iota(jnp.int32, (c, RET_V_DIM), 0).astype(F32)
    row_k = lax.broadcasted_iota(jnp.int32, (c, HEAD_DIM), 0).astype(F32)
    rel = (lax.broadcasted_iota(jnp.int32, (c, c), 0) - lax.broadcasted_iota(jnp.int32, (c, c), 1)).astype(F32)
    consts = []
    for hh in range(hs):
        lgf = _log_sigmoid(jnp.full((1, 1), dec_ref[0, h0 + hh], F32))
        lgb = _log_sigmoid(jnp.full((1, 1), dec_ref[1, h0 + hh], F32))
        consts.append(dict(
            decay=jnp.where(rel >= 0, jnp.exp(jnp.maximum(rel, 0.0) * lgf), jnp.exp(jnp.maximum(-rel, 0.0) * lgb)),
            qdec_f=jnp.exp((row + 1.0) * lgf),
            qdec_b=jnp.exp((c - row) * lgb),
            kdec_f=jnp.exp((c - 1.0 - row_k) * lgf),
            kdec_b=jnp.exp(row_k * lgb),
            chunk_f=jnp.exp(c * lgf), chunk_b=jnp.exp(c * lgb)))

    def load(ci, hh):
        rows = pl.ds(pl.multiple_of(ci * c, c), c)
        qk = slice(hh * HEAD_DIM, (hh + 1) * HEAD_DIM)
        vv = slice(hh * RET_V_DIM, (hh + 1) * RET_V_DIM)
        return rows, vv, q_ref[rows, qk], k_ref[rows, qk], v_ref[rows, vv]

    def scans(step, states):
        out, pending = [], []
        for hh, (state_f, state_b) in enumerate(states):
            cn = consts[hh]
            rows, vv, q, k, v = load(step, hh)
            a = lax.dot_general(q, k, TRANS_B, preferred_element_type=F32)
            cross = jnp.dot(q, state_f.astype(BF16), preferred_element_type=F32) * cn["qdec_f"]
            kd = (k.astype(F32) * cn["kdec_f"]).T.astype(BF16)
            state_f = state_f * cn["chunk_f"] + jnp.dot(kd, v, preferred_element_type=F32)

            rows1, _, q1, k1, v1 = load(nc - 1 - step, hh)
            accb_ref[rows1, vv] = jnp.dot(q1, state_b.astype(BF16), preferred_element_type=F32) * cn["qdec_b"]
            kd1 = (k1.astype(F32) * cn["kdec_b"]).T.astype(BF16)
            state_b = state_b * cn["chunk_b"] + jnp.dot(kd1, v1, preferred_element_type=F32)
            out.append((state_f, state_b))
            pending.append((rows, vv, (a * cn["decay"]).astype(BF16), v, cross))
        for rows, vv, a_dec, v, cross in pending:
            acc_ref[rows, vv] = jnp.dot(a_dec, v, preferred_element_type=F32) + cross
        return tuple(out)

    zero = jnp.zeros((HEAD_DIM, RET_V_DIM), F32)
    lax.fori_loop(0, nc, scans, ((zero, zero),) * hs)

    def finish(ci, carry):
        rows = pl.ds(pl.multiple_of(ci * c, c), c)
        for hh in range(hs):
            vv = slice(hh * RET_V_DIM, (hh + 1) * RET_V_DIM)
            y = acc_ref[rows, vv] + accb_ref[rows, vv]
            y = y * lax.rsqrt(jnp.mean(y * y, axis=-1, keepdims=True) + EPS)
            g = g_ref[rows, vv].astype(F32)
            o_ref[rows, vv] = (g * jax.nn.sigmoid(g) * y).astype(o_ref.dtype)
        return carry

    lax.fori_loop(0, nc, finish, 0)


def retention(ret, decays, batch, s):
    t = batch * s
    hs = RET_STEP_HEADS
    qw, vw = hs * HEAD_DIM, hs * RET_V_DIM
    qk = lambda base: pl.BlockSpec((s, qw), lambda b, h: (b, base // qw + h))
    vg = lambda base: pl.BlockSpec((s, vw), lambda b, h: (b, base // vw + h))
    return pl.pallas_call(
        functools.partial(_retention_kernel, seq=s),
        grid=(batch, RET_HEADS // hs),
        in_specs=[pl.BlockSpec(memory_space=pltpu.SMEM),
                  qk(0), qk(RET_QK_WIDTH), vg(2 * RET_QK_WIDTH), vg(2 * RET_QK_WIDTH + RET_V_WIDTH)],
        out_specs=pl.BlockSpec((s, vw), lambda b, h: (b, h)),
        out_shape=jax.ShapeDtypeStruct((t, RET_V_WIDTH), BF16),
        scratch_shapes=[pltpu.VMEM((s, vw), F32)] * 2,
        compiler_params=_params(("parallel", "parallel")), name="retention")(decays, ret, ret, ret, ret)


def _branch_merge_kernel(xn_ref, oa_ref, ob_ref, wga_ref, wgr_ref, wa_ref, wr_ref, ba_ref, br_ref, o_ref):
    xn = xn_ref[...]
    ga = jax.nn.sigmoid(jnp.dot(xn, wga_ref[...], preferred_element_type=F32) + ba_ref[...])
    gr = jax.nn.sigmoid(jnp.dot(xn, wgr_ref[...], preferred_element_type=F32) + br_ref[...])
    a = jnp.dot(oa_ref[...], wa_ref[...], preferred_element_type=F32)
    r = jnp.dot(ob_ref[...], wr_ref[...], preferred_element_type=F32)
    o_ref[...] = (ga * a + gr * r).astype(o_ref.dtype)


def branch_merge(xn, o_a, o_b, w_gate, b_gate, w_a, w_r, layer):
    t, d = xn.shape
    tm = _row_tile(t, 1024)
    tn = min(256, d)
    nb = d // tn
    res = lambda width: pl.BlockSpec((tm, width), lambda i, n: (i, 0))
    return pl.pallas_call(
        _branch_merge_kernel, grid=(t // tm, nb),
        in_specs=[res(d), res(o_a.shape[1]), res(o_b.shape[1]),
                  pl.BlockSpec((None, d, tn), lambda i, n: (layer, 0, n)),
                  pl.BlockSpec((None, d, tn), lambda i, n: (layer, 0, nb + n)),
                  pl.BlockSpec((None, o_a.shape[1], tn), lambda i, n: (layer, 0, n)),
                  pl.BlockSpec((None, o_b.shape[1], tn), lambda i, n: (layer, 0, n)),
                  pl.BlockSpec((None, 1, tn), lambda i, n: (layer, 0, n)),
                  pl.BlockSpec((None, 1, tn), lambda i, n: (layer, 0, nb + n))],
        out_specs=pl.BlockSpec((tm, tn), lambda i, n: (i, n)),
        out_shape=jax.ShapeDtypeStruct((t, d), BF16),
        compiler_params=_params(("parallel", "arbitrary")), name="branch_merge")(
            xn, o_a, o_b, w_gate, w_gate, w_a, w_r, b_gate, b_gate)


def _residual_pair_kernel(a_ref, b_ref, m_ref, w_ref, o_ref, *, n_first):
    x = jnp.where(pl.program_id(0) < n_first, a_ref[...], b_ref[...])
    o_ref[...] = x + jnp.dot(m_ref[...], w_ref[...], preferred_element_type=F32)


def _residual_sum_kernel(x_ref, d_ref, m_ref, w_ref, o_ref):
    o_ref[...] = (x_ref[...] + d_ref[...]) + jnp.dot(m_ref[...], w_ref[...], preferred_element_type=F32)


def residual_matmul(x_parts, merged, w_out, layer, stacked):
    xa, xb = x_parts
    t, d = merged.shape
    tm = _row_tile(math.gcd(xa.shape[0], t), 1024)
    tn = min(512, d)
    tile = pl.BlockSpec((tm, tn), lambda i, n: (i, n))
    if stacked:
        nf = xa.shape[0] // tm
        body = functools.partial(_residual_pair_kernel, n_first=nf)
        x_specs = _pair_specs(tm, tn, nf, lambda i, n: n)
    else:
        body = _residual_sum_kernel
        x_specs = [tile, tile]
    return pl.pallas_call(
        body, grid=(t // tm, d // tn),
        in_specs=x_specs + [pl.BlockSpec((tm, d), lambda i, n: (i, 0)),
                            pl.BlockSpec((None, d, tn), lambda i, n: (layer, 0, n))],
        out_specs=tile, out_shape=jax.ShapeDtypeStruct((t, d), F32),
        compiler_params=_params(("arbitrary", "arbitrary")), name="residual_matmul")(xa, xb, merged, w_out)


PEER_SEL_TM = 256
PEER_SEL_HEADS = 2
PEER_W_ROWS = 8
_BIG_INDEX = float(1 << 20)
_NSEL = PEER_HEADS * PEER_TOPK

_PAIR_CANDIDATES = [(a, b) for a in range(PEER_TOPK) for b in range(PEER_TOPK)
                    if (a + 1) * (b + 1) <= PEER_TOPK]


def _peer_query_kernel(x_ref, w_ref, o_ref):
    acc = jnp.dot(x_ref[...], w_ref[...], preferred_element_type=F32).astype(o_ref.dtype)
    for c in range(o_ref.shape[0]):
        o_ref[c] = acc[:, c * HEAD_DIM:(c + 1) * HEAD_DIM]


def peer_query(x, w, layer):
    t, d = x.shape
    n = w.shape[2]
    tm = _row_tile(t, 1024)
    tn = min(1024, n)
    return pl.pallas_call(
        _peer_query_kernel, grid=(t // tm, n // tn),
        in_specs=[pl.BlockSpec((tm, d), lambda i, j: (i, 0)),
                  pl.BlockSpec((None, d, tn), lambda i, j: (layer, 0, j))],
        out_specs=pl.BlockSpec((tn // HEAD_DIM, tm, HEAD_DIM), lambda i, j: (j, i, 0)),
        out_shape=jax.ShapeDtypeStruct((n // HEAD_DIM, t, HEAD_DIM), BF16),
        compiler_params=_params(("parallel", "arbitrary")), name="peer_query")(x, w)


def _topk_rows(x, idx, k):
    vals, idxs = [], []
    for _ in range(k):
        m = jnp.max(x, axis=0, keepdims=True)
        sel = jnp.min(jnp.where(x == m, idx, _BIG_INDEX), axis=0, keepdims=True)
        vals.append(m)
        idxs.append(sel)
        x = jnp.where(idx == sel, -jnp.inf, x)
    return vals, idxs


def _peer_select_kernel(q_ref, keys_ref, w_ref, sel_t, sel_rows):
    step = pl.program_id(0)
    tm = q_ref.shape[1]
    k = PEER_TOPK
    slot = step % 2
    prev = 1 - slot
    per_iter = tm * PEER_SEL_HEADS // PEER_HEADS

    @pl.when(step == 0)
    def _():
        sel_rows[1] = jnp.zeros(sel_rows.shape[1:], F32)

    key_idx = lax.broadcasted_iota(jnp.int32, (N_KEYS, tm), 0).astype(F32)
    ncand = len(_PAIR_CANDIDATES)
    npad = -(-ncand // 8) * 8
    cand_row = lax.broadcasted_iota(jnp.int32, (npad, tm), 0)
    cand_idx = jnp.full((npad, tm), _BIG_INDEX, F32)
    for r, (a, b) in enumerate(_PAIR_CANDIDATES):
        cand_idx = jnp.where(cand_row == r, float(a * k + b), cand_idx)
    pad_rows = [jnp.full((1, tm), -jnp.inf, F32)] * (npad - ncand)
    sub = lax.broadcasted_iota(jnp.int32, (N_KEYS, _NSEL), 0).astype(F32)

    def select_head(h):
        tops = []
        for p in range(2):
            sc = lax.dot_general(keys_ref[h, p], q_ref[2 * h + p], TRANS_B, preferred_element_type=F32)
            tops.append(_topk_rows(sc, key_idx, k))
        (v0, i0), (v1, i1) = tops
        cand = jnp.concatenate([v0[a] + v1[b] for a, b in _PAIR_CANDIDATES] + pad_rows, axis=0)
        sel_s, sel_pos = _topk_rows(cand, cand_idx, k)
        sel_s = jnp.concatenate(sel_s, axis=0)
        sel_pos = jnp.concatenate(sel_pos, axis=0)
        a_sel = jnp.floor(sel_pos * (1.0 / k))
        b_sel = sel_pos - a_sel * k
        i_sel = jnp.zeros((k, tm), F32)
        j_sel = jnp.zeros((k, tm), F32)
        for r in range(k):
            i_sel = i_sel + jnp.where(a_sel == r, i0[r], 0.0)
            j_sel = j_sel + jnp.where(b_sel == r, i1[r], 0.0)
        e = jnp.exp(sel_s - sel_s[0:1, :])
        rows = pl.ds(pl.multiple_of(h * k, k), k)
        sel_t[0, rows, :] = i_sel
        sel_t[1, rows, :] = j_sel
        sel_t[2, rows, :] = e / jnp.sum(e, axis=0, keepdims=True)

    def build_mask(tk):
        row = pl.ds(tk, 1)
        eq_i = sub == sel_rows[prev, 0, row, :]
        eq_j = sub == sel_rows[prev, 1, row, :]
        lhs = jnp.concatenate([jnp.where(eq_i, sel_rows[prev, 2, row, :], 0.0),
                               jnp.where(eq_i, sel_rows[prev, 3, row, :], 0.0)], axis=1).astype(BF16)
        one = jnp.where(eq_j, 1.0, 0.0).astype(BF16)
        rhs = jnp.concatenate([one, one], axis=1)
        wt = lax.dot_general(lhs, rhs, TRANS_B, preferred_element_type=F32)
        w_ref[:, tk] = wt.reshape(N_KEYS // PEER_W_ROWS, PEER_W_ROWS, N_KEYS)

    def heads(it, carry):
        for hh in range(PEER_SEL_HEADS):
            select_head(it * PEER_SEL_HEADS + hh)
        for u in range(per_iter):
            build_mask(it * per_iter + u)
        return carry

    lax.fori_loop(0, PEER_HEADS // PEER_SEL_HEADS, heads, 0)

    gate = sel_t[2].T
    gate_hi = gate.astype(BF16).astype(F32)
    sel_rows[slot, 0] = sel_t[0].T
    sel_rows[slot, 1] = sel_t[1].T
    sel_rows[slot, 2] = gate_hi
    sel_rows[slot, 3] = gate - gate_hi


def peer_select(q, sub_keys, layer):
    nq, t, _ = q.shape
    tm = _row_tile(t, PEER_SEL_TM)
    nt = t // tm
    nib = N_KEYS // PEER_W_ROWS
    return pl.pallas_call(
        _peer_select_kernel, grid=(nt + 1,),
        in_specs=[pl.BlockSpec((nq, tm, HEAD_DIM), lambda s: (0, jnp.minimum(s, nt - 1), 0)),
                  pl.BlockSpec((None,) + sub_keys.shape[1:], lambda s: (layer, 0, 0, 0, 0))],
        out_specs=pl.BlockSpec((nib, tm, PEER_W_ROWS, N_KEYS), lambda s: (0, jnp.maximum(s - 1, 0), 0, 0)),
        out_shape=jax.ShapeDtypeStruct((nib, t, PEER_W_ROWS, N_KEYS), F32),
        scratch_shapes=[pltpu.VMEM((3, _NSEL, tm), F32), pltpu.VMEM((2, 4, tm, _NSEL), F32)],
        compiler_params=_params(("arbitrary",)), name="peer_select")(q, sub_keys)


PEER_TE = 512
PEER_TM = 1024


def _gelu(x):
    return 0.5 * x * (1.0 + lax.erf(x * math.sqrt(0.5)))


def _peer_experts_kernel(xn_ref, u_ref, v_ref, w_ref, o_ref, mix_scr, *, ne):
    s = pl.program_id(0)
    tm = xn_ref.shape[0]
    sub_blocks = PEER_TE // N_KEYS
    slot = s % 2
    e = s % ne

    @pl.when(s == 0)
    def _():
        o_ref[...] = jnp.zeros_like(o_ref)
        mix_scr[1] = jnp.zeros(mix_scr.shape[1:], mix_scr.dtype)

    starts_tile = jnp.logical_or(s == 0, (s - 1) % ne == 0)
    prev = jnp.where(starts_tile, 0.0, o_ref[...])
    o_ref[...] = prev + jnp.dot(mix_scr[1 - slot], v_ref[...], preferred_element_type=F32)

    h = lax.dot_general(xn_ref[...], u_ref[...], TRANS_B, preferred_element_type=F32)
    base = (e % (PEER_W_ROWS // sub_blocks)) * sub_blocks
    parts = []
    for ii in range(sub_blocks):
        wsub = w_ref[pl.ds(base + ii, tm, stride=PEER_W_ROWS), :]
        parts.append((wsub * _gelu(h[:, ii * N_KEYS:(ii + 1) * N_KEYS])).astype(BF16))
    mix_scr[slot] = jnp.concatenate(parts, axis=1)


def peer_experts(xn, u, v, w, layer):
    t, d = xn.shape
    ne = u.shape[1] // PEER_TE
    tm = _row_tile(t, PEER_TM)
    per_w = PEER_W_ROWS * N_KEYS // PEER_TE
    nt = t // tm
    steps = nt * ne
    w2 = w.reshape(-1, N_KEYS)
    cur = lambda s: jnp.minimum(s, steps - 1)
    last = lambda s: jnp.maximum(s - 1, 0)
    once = pl.Buffered(1)
    return pl.pallas_call(
        functools.partial(_peer_experts_kernel, ne=ne), grid=(steps + 1,),
        in_specs=[pl.BlockSpec((tm, d), lambda s: (cur(s) // ne, 0), pipeline_mode=once),
                  pl.BlockSpec((None, PEER_TE, d), lambda s: (layer, cur(s) % ne, 0)),
                  pl.BlockSpec((None, PEER_TE, d), lambda s: (layer, last(s) % ne, 0)),
                  pl.BlockSpec((tm * PEER_W_ROWS, N_KEYS),
                               lambda s: (((cur(s) % ne) // per_w) * nt + cur(s) // ne, 0))],
        out_specs=pl.BlockSpec((tm, d), lambda s: (last(s) // ne, 0), pipeline_mode=once),
        out_shape=jax.ShapeDtypeStruct((t, d), F32),
        scratch_shapes=[pltpu.VMEM((2, tm, PEER_TE), BF16)],
        compiler_params=_params(("arbitrary",)), name="peer_experts")(xn, u, v, w2)


def kernel(x_prompt, x_sample, norm_mix, w_in, ret_decay_fwd, ret_decay_bwd, w_branch_attn, w_branch_ret,
           w_gate, b_gate, w_out, norm_ffn, peer_w_q, peer_sub_keys, peer_u, peer_v, final_norm):
    assert x_prompt.shape[1:] == x_sample.shape[1:]
    bp, s, d = x_prompt.shape
    bs = x_sample.shape[0]
    batch = bp + bs
    t = batch * s
    depth = w_in.shape[0]
    assert w_in.shape[2] == IN_COLS and s % (ATT_DILATIONS[-1] * 2 * ATT_HALF_WINDOW) == 0

    w_in, w_gate, w_out, w_a, w_r, w_q, u, v = (
        cast_bf16(w) for w in (w_in, w_gate, w_out, w_branch_attn, w_branch_ret, peer_w_q, peer_u, peer_v))
    sub_keys = peer_sub_keys.astype(BF16)
    b_gate = b_gate.astype(F32).reshape(depth, 1, -1)
    tables = _rotation_tables(s)
    x_parts = (x_prompt.reshape(bp * s, d), x_sample.reshape(bs * s, d))
    for l in range(depth):
        if l == 0:
            xn = rmsnorm_pair(*x_parts, norm_mix[l])
        else:
            xn = add_rmsnorm(*x_parts, norm_mix[l])
        att, ret = input_projection(xn, w_in, l, tables, s)
        o_a = dilated_attention(att, batch, s)
        decays = jnp.stack([ret_decay_fwd[l], ret_decay_bwd[l]]).astype(F32)
        o_b = retention(ret, decays, batch, s)
        merged = branch_merge(xn, o_a, o_b, w_gate, b_gate, w_a, w_r, l)
        x = residual_matmul(x_parts, merged, w_out, l, stacked=(l == 0))
        xn2 = rmsnorm(x, norm_ffn[l], BF16)
        q = peer_query(xn2, w_q, l)
        w = peer_select(q, sub_keys, l)
        x_parts = (x, peer_experts(xn2, u, v, w, l))
    y_p, y_s = add_rmsnorm_split(*x_parts, final_norm, bp * s)
    return y_p.reshape(bp, s, d), y_s.reshape(bs, s, d)
```
